```python
import jax, jax.numpy as jnp
from jax import lax
import numpy as np

D_MODEL = 1024
BATCH = 8
SEQ = 4096
DEPTH = 2

GRID_W = 64
CTX_LEN = 256
EPS = 1e-6
ROPE_BASE = 10000.0

MLA_HEADS = 8
MLA_Q_RANK = D_MODEL // 4
MLA_KV_RANK = D_MODEL // 8
MLA_NOPE = 64
MLA_ROPE = 32
MLA_V = 64
MLA_IN = MLA_Q_RANK + MLA_KV_RANK + MLA_ROPE
MLA_SCALE = (MLA_NOPE + MLA_ROPE) ** -0.5
Q_BLOCK = 128

CONV_DIM = D_MODEL // 2
CONV_WIDTH = 3
EVEN_IN = MLA_IN + 3 * CONV_DIM
EVEN_MIX = MLA_HEADS * MLA_V + CONV_DIM

ML_HEADS = 4
ML_DQK = D_MODEL // 8
ML_DV = D_MODEL // 4
ML_CHUNK = 64
ML_QKVG = ML_HEADS * (2 * ML_DQK + ML_DV) + 4 * ML_HEADS
ML_MIX = ML_HEADS * ML_DV
ODD_IN = ML_QKVG + ML_MIX
F_BIAS = 3.0

PEER_HEADS = 8
PEER_NKEYS = 128
PEER_N = PEER_NKEYS * PEER_NKEYS
PEER_DKEY = 256
PEER_DHALF = PEER_DKEY // 2
PEER_TOPK = 16
PEER_BLOCK = 128

kernel_name = 'hybrid_mla_conv_mlstm_peer_dit'


def rmsnorm(x, g):
    x32 = x.astype(jnp.float32)
    y = x32 * lax.rsqrt(jnp.mean(x32 * x32, axis=-1, keepdims=True) + EPS)
    return (y * g.astype(jnp.float32)).astype(x.dtype)


def modulate(h, shift, scale):
    return h * (1 + scale) + shift


def axial_rope(rows, dim):
    n_freq = dim // 4
    inv = ROPE_BASE ** (-jnp.arange(n_freq, dtype=jnp.float32) / n_freq)
    row = jnp.repeat(jnp.arange(rows, dtype=jnp.float32), GRID_W)
    col = jnp.tile(jnp.arange(GRID_W, dtype=jnp.float32), rows)
    ang = jnp.concatenate([row[:, None] * inv, col[:, None] * inv], axis=-1)
    return jnp.cos(ang), jnp.sin(ang)


def apply_rope(x, cos, sin):
    half = x.shape[-1] // 2
    x32 = x.astype(jnp.float32)
    x1, x2 = x32[..., :half], x32[..., half:]
    return jnp.concatenate([x1 * cos - x2 * sin, x1 * sin + x2 * cos], axis=-1).astype(x.dtype)


def mla_q(p_q, q_norm, w_uq, cos, sin):
    B, T, _ = p_q.shape
    q = (rmsnorm(p_q, q_norm) @ w_uq).reshape(B, T, MLA_HEADS, MLA_NOPE + MLA_ROPE)
    q_n, q_r = q[..., :MLA_NOPE], q[..., MLA_NOPE:]
    if cos is not None:
        q_r = apply_rope(q_r, cos[:, None, :], sin[:, None, :])
    return q_n, q_r


def mla_kv(p_kv, kv_norm, w_ukv, cos, sin):
    B, T, _ = p_kv.shape
    kv = (rmsnorm(p_kv[..., :MLA_KV_RANK], kv_norm) @ w_ukv).reshape(B, T, MLA_HEADS, MLA_NOPE + MLA_V)
    k_r = p_kv[..., MLA_KV_RANK:]
    if cos is not None:
        k_r = apply_rope(k_r, cos, sin)
    return kv[..., :MLA_NOPE], k_r, kv[..., MLA_NOPE:]


def mla_attend(q_n, q_r, k_n, k_r, v):
    s = jnp.einsum('bqhd,bkhd->bhqk', q_n, k_n) + jnp.einsum('bqhr,bkr->bhqk', q_r, k_r)
    p = jax.nn.softmax(s.astype(jnp.float32) * MLA_SCALE, axis=-1).astype(v.dtype)
    out = jnp.einsum('bhqk,bkhd->bqhd', p, v)
    return out.reshape(out.shape[0], out.shape[1], MLA_HEADS * MLA_V)


def short_conv(p, conv_w):
    T = p.shape[1]
    b_gate, c_gate, u = p[..., :CONV_DIM], p[..., CONV_DIM:2 * CONV_DIM], p[..., 2 * CONV_DIM:]
    z = jnp.pad(c_gate * u, ((0, 0), (1, 1), (0, 0)))
    y = conv_w[0] * z[:, :T] + conv_w[1] * z[:, 1:T + 1] + conv_w[2] * z[:, 2:]
    return b_gate * y


def even_mixer(nc, nl, w_in, q_norm, kv_norm, w_uq, w_ukv, conv_w, w_out, cos, sin, need_ctx):
    B, S, _ = nl.shape
    pl = nl @ w_in
    if need_ctx:
        pc = nc @ w_in
        pc_kv = pc[..., MLA_Q_RANK:MLA_IN]
    else:
        pc_kv = nc @ w_in[:, MLA_Q_RANK:MLA_IN]
    qn_l, qr_l = mla_q(pl[..., :MLA_Q_RANK], q_norm, w_uq, cos, sin)
    kn_l, kr_l, v_l = mla_kv(pl[..., MLA_Q_RANK:MLA_IN], kv_norm, w_ukv, cos, sin)
    kn_c, kr_c, v_c = mla_kv(pc_kv, kv_norm, w_ukv, None, None)
    kn = jnp.concatenate([kn_l, kn_c], axis=1)
    kr = jnp.concatenate([kr_l, kr_c], axis=1)
    vv = jnp.concatenate([v_l, v_c], axis=1)
    blocks = lambda a: jnp.moveaxis(a.reshape((B, S // Q_BLOCK, Q_BLOCK) + a.shape[2:]), 1, 0)
    ol = lax.map(lambda qb: mla_attend(qb[0], qb[1], kn, kr, vv), (blocks(qn_l), blocks(qr_l)))
    ol = jnp.moveaxis(ol, 0, 1).reshape(B, S, MLA_HEADS * MLA_V)
    yl = jnp.concatenate([ol, short_conv(pl[..., MLA_IN:], conv_w)], axis=-1) @ w_out
    yc = None
    if need_ctx:
        qn_c, qr_c = mla_q(pc[..., :MLA_Q_RANK], q_norm, w_uq, None, None)
        oc = mla_attend(qn_c, qr_c, kn_c, kr_c, v_c)
        yc = jnp.concatenate([oc, short_conv(pc[..., MLA_IN:], conv_w)], axis=-1) @ w_out
    return yc, yl


def mlstm_inputs(p, gate_b):
    B, T, _ = p.shape
    qk, vd = ML_HEADS * ML_DQK, ML_HEADS * ML_DV
    heads = lambda a, d: a.reshape(B, T, ML_HEADS, d).transpose(0, 2, 1, 3)
    q = heads(p[..., :qk], ML_DQK) * (ML_DQK ** -0.5)
    k = heads(p[..., qk:2 * qk], ML_DQK)
    v = heads(p[..., 2 * qk:2 * qk + vd], ML_DV)
    g = p[..., 2 * qk + vd:ML_QKVG].astype(jnp.float32) + gate_b.astype(jnp.float32)
    g = g.reshape(B, T, 4, ML_HEADS).transpose(2, 0, 3, 1)
    return q, k, v, g


def mlstm_scan(q, k, v, ig, lf, state, with_output):
    B, H, T, _ = q.shape
    L = ML_CHUNK
    nc = T // L
    chunks = lambda a: jnp.moveaxis(a.reshape(a.shape[:2] + (nc, L) + a.shape[3:]), 2, 0)
    xs = (chunks(q), chunks(k), chunks(v), chunks(ig), chunks(lf))
    causal = jnp.tril(jnp.ones((L, L), dtype=bool))

    def step(carry, xc):
        C, n, m = carry
        qc, kc, vc, ic, fc = xc
        b = jnp.cumsum(fc, axis=-1)
        b_end = b[..., -1]
        a = ic + b_end[..., None] - b
        m_new = jnp.maximum(b_end + m, jnp.max(a, axis=-1))
        w_s = jnp.exp(a - m_new[..., None])
        decay = jnp.exp(b_end + m - m_new)
        C_new = decay[..., None, None] * C + jnp.einsum('bhs,bhsk,bhsv->bhkv', w_s, kc, vc)
        n_new = decay[..., None] * n + jnp.einsum('bhs,bhsk->bhk', w_s, kc)
        if with_output:
            dmat = jnp.where(causal, b[..., :, None] - b[..., None, :] + ic[..., None, :], -jnp.inf)
            m_t = jnp.maximum(b + m[..., None], jnp.max(dmat, axis=-1))
            inter = jnp.exp(b + m[..., None] - m_t)
            sc = jnp.einsum('bhtk,bhsk->bhts', qc, kc) * jnp.exp(dmat - m_t[..., None])
            num = jnp.einsum('bhts,bhsv->bhtv', sc, vc) + inter[..., None] * jnp.einsum('bhtk,bhkv->bhtv', qc, C)
            den = jnp.sum(sc, axis=-1) + inter * jnp.einsum('bhtk,bhk->bht', qc, n)
            h = num / jnp.maximum(jnp.abs(den), jnp.exp(-m_t))[..., None]
        else:
            h = None
        return (C_new, n_new, m_new), h

    state, hs = lax.scan(step, state, xs)
    if with_output:
        hs = jnp.moveaxis(hs, 0, 2).reshape(B, H, T, ML_DV)
    return state, hs


def odd_mixer(nc, nl, w_in, gate_b, head_g, w_out, need_ctx):
    pc = nc @ (w_in if need_ctx else w_in[:, :ML_QKVG])
    pl = nl @ w_in
    qc, kc, vc, gc = mlstm_inputs(pc, gate_b)
    ql, kl, vl, gl = mlstm_inputs(pl, gate_b)
    B = nl.shape[0]
    zero = (jnp.zeros((B, ML_HEADS, ML_DQK, ML_DV), jnp.float32),
            jnp.zeros((B, ML_HEADS, ML_DQK), jnp.float32),
            jnp.zeros((B, ML_HEADS), jnp.float32))
    flip = lambda a: jnp.flip(a, axis=2)
    h_c, h_l = None, None
    for d in range(2):
        ctx_seq = (qc, kc, vc, gc[2 * d], jax.nn.log_sigmoid(gc[2 * d + 1]))
        lat_seq = (ql, kl, vl, gl[2 * d], jax.nn.log_sigmoid(gl[2 * d + 1]))
        if d == 1:
            ctx_seq = tuple(flip(a) for a in ctx_seq)
            lat_seq = tuple(flip(a) for a in lat_seq)
        state, hc_d = mlstm_scan(*ctx_seq, zero, need_ctx)
        _, hl_d = mlstm_scan(*lat_seq, state, True)
        if d == 1:
            hl_d = flip(hl_d)
            hc_d = flip(hc_d) if need_ctx else None
        h_l = hl_d if h_l is None else h_l + hl_d
        if need_ctx:
            h_c = hc_d if h_c is None else h_c + hc_d

    def readout(h, p):
        Bh, _, T, _ = h.shape
        h = h.transpose(0, 2, 1, 3).astype(p.dtype)
        h = rmsnorm(h, head_g.reshape(ML_HEADS, ML_DV)).reshape(Bh, T, ML_MIX)
        return (jax.nn.sigmoid(p[..., ML_QKVG:]) * h) @ w_out

    yl = readout(h_l, pl)
    yc = readout(h_c, pc) if need_ctx else None
    return yc, yl


def peer(h, w_q, subkeys, u, v):
    B, T, D = h.shape
    tok = h.reshape(-1, PEER_BLOCK, D)

    def block(xb):
        q = (xb @ w_q).reshape(PEER_BLOCK, PEER_HEADS, 2, PEER_DHALF)
        s = jnp.einsum('thpd,hpkd->thpk', q, subkeys).astype(jnp.float32)
        s1, i1 = lax.top_k(s[:, :, 0], PEER_TOPK)
        s2, i2 = lax.top_k(s[:, :, 1], PEER_TOPK)
        cand = (s1[..., :, None] + s2[..., None, :]).reshape(PEER_BLOCK, PEER_HEADS, PEER_TOPK * PEER_TOPK)
        cidx = (i1[..., :, None] * PEER_NKEYS + i2[..., None, :]).reshape(PEER_BLOCK, PEER_HEADS, PEER_TOPK * PEER_TOPK)
        top, pos = lax.top_k(cand, PEER_TOPK)
        idx = jnp.take_along_axis(cidx, pos, axis=-1)
        g = jax.nn.softmax(top, axis=-1)
        ue, ve = u[idx], v[idx]
        act = jax.nn.gelu(jnp.einsum('td,thkd->thk', xb, ue).astype(jnp.float32), approximate=False)
        return jnp.einsum('thk,thkd->td', (g * act).astype(xb.dtype), ve)

    return lax.map(block, tok).reshape(B, T, D)


def setup_inputs(seed: int = 0) -> dict:
    key = jax.random.key(seed)
    ks = iter(jax.random.split(key, 32))
    nrm = lambda shape, s: jax.random.normal(next(ks), shape, jnp.float32) * s
    n_even = (DEPTH + 1) // 2
    n_odd = DEPTH // 2
    D = D_MODEL
    gate_b = (nrm((n_odd, 4, ML_HEADS), 0.1)
              + jnp.array([0.0, F_BIAS, 0.0, F_BIAS], jnp.float32)[:, None]).reshape(n_odd, 4 * ML_HEADS)
    return {
        'x': nrm((BATCH, SEQ, D), 1.0),
        'c': nrm((BATCH, D), 1.0),
        'ctx': nrm((BATCH, CTX_LEN, D), 1.0),
        'c_ctx': nrm((D,), 1.0),
        'norm1_g': 1.0 + nrm((DEPTH, D), 0.02),
        'norm2_g': 1.0 + nrm((DEPTH, D), 0.02),
        'w_mod': nrm((DEPTH, D, 6 * D), 0.5 * D ** -0.5),
        'b_mod': nrm((DEPTH, 6 * D), 0.02),
        'even_w_in': nrm((n_even, D, EVEN_IN), D ** -0.5),
        'mla_q_norm': 1.0 + nrm((n_even, MLA_Q_RANK), 0.02),
        'mla_kv_norm': 1.0 + nrm((n_even, MLA_KV_RANK), 0.02),
        'mla_w_uq': nrm((n_even, MLA_Q_RANK, MLA_HEADS * (MLA_NOPE + MLA_ROPE)), MLA_Q_RANK ** -0.5),
        'mla_w_ukv': nrm((n_even, MLA_KV_RANK, MLA_HEADS * (MLA_NOPE + MLA_V)), MLA_KV_RANK ** -0.5),
        'conv_w': nrm((n_even, CONV_WIDTH, CONV_DIM), CONV_WIDTH ** -0.5),
        'even_w_out': nrm((n_even, EVEN_MIX, D), EVEN_MIX ** -0.5),
        'odd_w_in': nrm((n_odd, D, ODD_IN), D ** -0.5),
        'mlstm_gate_b': gate_b,
        'mlstm_head_g': 1.0 + nrm((n_odd, ML_MIX), 0.02),
        'odd_w_out': nrm((n_odd, ML_MIX, D), ML_MIX ** -0.5),
        'peer_w_q': nrm((DEPTH, D, PEER_HEADS * PEER_DKEY), D ** -0.5),
        'peer_subkeys': nrm((DEPTH, PEER_HEADS, 2, PEER_NKEYS, PEER_DHALF), PEER_DHALF ** -0.5),
        'peer_u': nrm((DEPTH, PEER_N, D), D ** -0.5),
        'peer_v': nrm((DEPTH, PEER_N, D), PEER_HEADS ** -0.5),
        'norm_f_g': 1.0 + nrm((D,), 0.02),
    }


def reference(x, c, ctx, c_ctx, norm1_g, norm2_g, w_mod, b_mod,
              even_w_in, mla_q_norm, mla_kv_norm, mla_w_uq, mla_w_ukv, conv_w, even_w_out,
              odd_w_in, mlstm_gate_b, mlstm_head_g, odd_w_out,
              peer_w_q, peer_subkeys, peer_u, peer_v, norm_f_g):
    B, S, D = x.shape
    rows = S // GRID_W
    cos, sin = axial_rope(rows, MLA_ROPE)
    hl, hc = x, ctx
    n_ctx = ctx.shape[1]
    for i in range(DEPTH):
        last = i == DEPTH - 1
        j = i // 2
        mod_l = [m[:, None, :] for m in jnp.split(jax.nn.silu(c) @ w_mod[i] + b_mod[i], 6, axis=-1)]
        mod_c = jnp.split(jax.nn.silu(c_ctx) @ w_mod[i] + b_mod[i], 6, axis=-1)
        nl = modulate(rmsnorm(hl, norm1_g[i]), mod_l[0], mod_l[1])
        nc = modulate(rmsnorm(hc, norm1_g[i]), mod_c[0], mod_c[1])
        if i % 2 == 0:
            yc, yl = even_mixer(nc, nl, even_w_in[j], mla_q_norm[j], mla_kv_norm[j], mla_w_uq[j],
                                mla_w_ukv[j], conv_w[j], even_w_out[j], cos, sin, not last)
        else:
            yc, yl = odd_mixer(nc, nl, odd_w_in[j], mlstm_gate_b[j], mlstm_head_g[j], odd_w_out[j], not last)
        hl = hl + mod_l[2] * yl
        nl = modulate(rmsnorm(hl, norm2_g[i]), mod_l[3], mod_l[4])
        if last:
            hl = hl + mod_l[5] * peer(nl, peer_w_q[i], peer_subkeys[i], peer_u[i], peer_v[i])
        else:
            hc = hc + mod_c[2] * yc
            nc = modulate(rmsnorm(hc, norm2_g[i]), mod_c[3], mod_c[4])
            ff = peer(jnp.concatenate([nc, nl], axis=1), peer_w_q[i], peer_subkeys[i], peer_u[i], peer_v[i])
            hc = hc + mod_c[5] * ff[:, :n_ctx]
            hl = hl + mod_l[5] * ff[:, n_ctx:]
    return rmsnorm(hl, norm_f_g)
```

```python
import functools

import jax
import jax.numpy as jnp
from jax import lax
from jax.experimental import pallas as pl
from jax.experimental.pallas import tpu as pltpu

GRID_W = 64
EPS = 1e-6
ROPE_BASE = 10000.0
MLA_HEADS = 8
MLA_NOPE = 64
MLA_ROPE = 32
MLA_V = 64
ML_HEADS = 4
ML_CHUNK = 64
PEER_TOPK = 16

F32 = jnp.float32
BF16 = jnp.bfloat16
LANE = 128
VMEM_LIMIT = 56 * 1024 * 1024
NEG_INF = float("-inf")
POS_INF = float("inf")


def _cparams(*sem):
    return pltpu.CompilerParams(dimension_semantics=sem, vmem_limit_bytes=VMEM_LIMIT)


def _tile(n, pref):
    t = min(n, pref)
    assert n % t == 0, (n, pref)
    return t


def _rms(x, g):
    return x * lax.rsqrt(jnp.mean(x * x, axis=-1, keepdims=True) + EPS) * g


def _norm_mod(h_ref, g_ref, sh_ref, sc_ref):
    return _rms(h_ref[0], g_ref[...]) * (1.0 + sc_ref[0]) + sh_ref[0]


def _mm(a, b):
    return jnp.dot(a, b, preferred_element_type=F32)


def _mm_nt(a, b):
    return lax.dot_general(a, b, (((1,), (1,)), ((), ())), preferred_element_type=F32)


def _mod_kernel(c_ref, w_ref, b_ref, o_ref):
    c = c_ref[...]
    o_ref[...] = _mm((c * jax.nn.sigmoid(c)).astype(BF16), w_ref[...]) + b_ref[...]


def _mod_vectors(cc, w, b):
    R, D = cc.shape
    N = w.shape[1]
    tn = _tile(N, 1024)
    return pl.pallas_call(
        _mod_kernel, grid=(N // tn,),
        in_specs=[pl.BlockSpec((R, D), lambda j: (0, 0)),
                  pl.BlockSpec((D, tn), lambda j: (0, j)),
                  pl.BlockSpec((1, tn), lambda j: (0, j))],
        out_specs=pl.BlockSpec((R, tn), lambda j: (0, j)),
        out_shape=jax.ShapeDtypeStruct((R, N), F32),
        compiler_params=_cparams("parallel"), name="mod_vectors")(cc, w, b)


def _bvec(v):
    v3 = v[:, None, :]
    if v3.shape[0] == 1:
        return v3, (lambda b, *_: (0, 0, 0))
    return v3, (lambda b, *_: (b, 0, 0))


def _even_in_kernel(h_ref, g_ref, sh_ref, sc_ref, win_ref, qn_ref, kvn_ref, wq2_ref, wkv2_ref, cos_ref, sin_ref,
                    q_ref, k_ref, v_ref, bg_ref, z_ref, *, q_rank, kv_rank, conv_dim, q_scale):
    nl = _norm_mod(h_ref, g_ref, sh_ref, sc_ref)
    p = _mm(nl.astype(BF16), win_ref[...])
    o = q_rank + kv_rank
    pq, pc = p[:, :q_rank], p[:, q_rank:o]
    kr, krr = p[:, o:o + LANE], p[:, o + LANE:o + 2 * LANE]
    o += 2 * LANE
    bg, cg, u = p[:, o:o + conv_dim], p[:, o + conv_dim:o + 2 * conv_dim], p[:, o + 2 * conv_dim:o + 3 * conv_dim]
    cos, sin = cos_ref[...], sin_ref[...]
    hw = MLA_HEADS * LANE
    cos_t, sin_t = jnp.tile(cos, (1, MLA_HEADS)), jnp.tile(sin, (1, MLA_HEADS))
    qq = _mm(_rms(pq, qn_ref[...]).astype(BF16), wq2_ref[...])
    q_ref[0] = ((qq[:, :hw] * cos_t + qq[:, hw:] * sin_t) * q_scale).astype(BF16)
    kv = _mm(_rms(pc, kvn_ref[...]).astype(BF16), wkv2_ref[...])
    k_ref[0] = (kv[:, :hw] + jnp.tile(kr * cos + krr * sin, (1, MLA_HEADS))).astype(BF16)
    v_ref[0] = kv[:, hw:].astype(BF16)
    bg_ref[0] = bg
    z_ref[0] = cg * u


def _even_in(h, g, shift, scale, win, qn, kvn, wq2, wkv2, cos, sin, *, conv_dim, q_scale):
    B, T, D = h.shape
    tm = _tile(T, 512)
    hw = MLA_HEADS * LANE
    sh3, vmap_ = _bvec(shift)
    sc3, _ = _bvec(scale)
    const = lambda b, i: (0, 0)
    row = lambda b, i: (b, i, 0)
    kern = functools.partial(_even_in_kernel, q_rank=qn.shape[1], kv_rank=kvn.shape[1], conv_dim=conv_dim,
                             q_scale=q_scale)
    return pl.pallas_call(
        kern, grid=(B, T // tm),
        in_specs=[pl.BlockSpec((1, tm, D), row), pl.BlockSpec(g.shape, const),
                  pl.BlockSpec((1, 1, D), vmap_), pl.BlockSpec((1, 1, D), vmap_),
                  pl.BlockSpec(win.shape, const), pl.BlockSpec(qn.shape, const), pl.BlockSpec(kvn.shape, const),
                  pl.BlockSpec(wq2.shape, const), pl.BlockSpec(wkv2.shape, const),
                  pl.BlockSpec((tm, LANE), lambda b, i: (i, 0)), pl.BlockSpec((tm, LANE), lambda b, i: (i, 0))],
        out_specs=[pl.BlockSpec((1, tm, hw), row)] * 3 + [pl.BlockSpec((1, tm, conv_dim), row)] * 2,
        out_shape=[jax.ShapeDtypeStruct((B, T, hw), BF16)] * 3 + [jax.ShapeDtypeStruct((B, T, conv_dim), F32)] * 2,
        compiler_params=_cparams("parallel", "parallel"), name="even_in")(
            h, g, sh3, sc3, win, qn, kvn, wq2, wkv2, cos, sin)


def _attn_kernel(q_ref, *refs, nsets):
    k_refs, v_refs, o_ref = refs[:nsets], refs[nsets:2 * nsets], refs[2 * nsets]
    q = q_ref[0]
    ss = [_mm_nt(q, k[0]) for k in k_refs]
    m = functools.reduce(jnp.maximum, [jnp.max(s, axis=-1, keepdims=True) for s in ss])
    ps = [jnp.exp(s - m) for s in ss]
    l = functools.reduce(jnp.add, [jnp.sum(p, axis=-1, keepdims=True) for p in ps])
    o = functools.reduce(jnp.add, [_mm(p.astype(BF16), v[0]) for p, v in zip(ps, v_refs)])
    o_ref[0] = (o / l).astype(BF16)


def _attention(q, ks, vs):
    B, T, _ = q.shape
    tq = _tile(T, 256)
    qmap = lambda b, h, i: (b, i, h)
    kmap = lambda b, h, i: (b, 0, h)
    kspecs = [pl.BlockSpec((1, k.shape[1], LANE), kmap) for k in ks]
    return pl.pallas_call(
        functools.partial(_attn_kernel, nsets=len(ks)), grid=(B, MLA_HEADS, T // tq),
        in_specs=[pl.BlockSpec((1, tq, LANE), qmap)] + kspecs + kspecs,
        out_specs=pl.BlockSpec((1, tq, LANE), qmap),
        out_shape=jax.ShapeDtypeStruct(q.shape, BF16),
        compiler_params=_cparams("parallel", "parallel", "parallel"), name="mla_attention")(q, *ks, *vs)


def _even_out_kernel(a_ref, bg_ref, z_ref, zp_ref, zn_ref, h_ref, gate_ref, cw_ref, wa_ref, wc_ref, o_ref):
    i, last = pl.program_id(1), pl.num_programs(1) - 1
    z = z_ref[0]
    tm = z.shape[0]
    rows = lax.broadcasted_iota(jnp.int32, z.shape, 0)
    prev_row = zp_ref[0][7:8, :] * (i > 0).astype(F32)
    next_row = zn_ref[0][0:1, :] * (i < last).astype(F32)
    zm1 = jnp.where(rows == 0, prev_row, pltpu.roll(z, 1, 0))
    zp1 = jnp.where(rows == tm - 1, next_row, pltpu.roll(z, tm - 1, 0))
    cw = cw_ref[...]
    y = cw[0:1, :] * zm1 + cw[1:2, :] * z + cw[2:3, :] * zp1
    o = _mm(a_ref[0], wa_ref[...]) + _mm((bg_ref[0] * y).astype(BF16), wc_ref[...])
    o_ref[0] = h_ref[0] + gate_ref[0] * o


def _even_out(attn, bg, z, h, gate, cw, wa, wc):
    B, T, D = h.shape
    tm = _tile(T, 512)
    cd = z.shape[-1]
    nb8 = T // 8
    g3, gmap = _bvec(gate)
    row = lambda b, i: (b, i, 0)
    const = lambda b, i: (0, 0)
    return pl.pallas_call(
        _even_out_kernel, grid=(B, T // tm),
        in_specs=[pl.BlockSpec((1, tm, attn.shape[-1]), row), pl.BlockSpec((1, tm, cd), row),
                  pl.BlockSpec((1, tm, cd), row),
                  pl.BlockSpec((1, 8, cd), lambda b, i: (b, jnp.maximum(i * (tm // 8) - 1, 0), 0)),
                  pl.BlockSpec((1, 8, cd), lambda b, i: (b, jnp.minimum((i + 1) * (tm // 8), nb8 - 1), 0)),
                  pl.BlockSpec((1, tm, D), row), pl.BlockSpec((1, 1, D), gmap),
                  pl.BlockSpec(cw.shape, const), pl.BlockSpec(wa.shape, const), pl.BlockSpec(wc.shape, const)],
        out_specs=pl.BlockSpec((1, tm, D), row),
        out_shape=jax.ShapeDtypeStruct(h.shape, F32),
        compiler_params=_cparams("parallel", "parallel"), name="even_out")(
            attn, bg, z, z, z, h, g3, cw, wa, wc)


def _peer_topk_head(s1, s2, ab_scr):
    K = PEER_TOPK
    for base, s in ((0, s1), (K, s2)):
        work = s
        for r in range(K):
            m = jnp.max(work, axis=0, keepdims=True)
            ab_scr[base + r:base + r + 1, :] = m
            if r + 1 < K:
                work = jnp.where(work == m, NEG_INF, work)
    bgrp = [ab_scr[K:K + 8, :], ab_scr[K + 8:K + 16, :]]
    a_rows = [ab_scr[i:i + 1, :] for i in range(K)]
    groups = []
    for i in range(K):
        for bj in (bgrp if i < 2 else bgrp[:1]):
            groups.append(a_rows[i] + bj)
    top = a_rows[0] + ab_scr[K:K + 1, :]
    zsum = jnp.zeros_like(top)
    tau = top
    work = groups
    for r in range(K):
        m = jnp.max(functools.reduce(jnp.maximum, work), axis=0, keepdims=True)
        zsum = zsum + jnp.exp(m - top)
        tau = m
        if r + 1 < K:
            work = [jnp.where(g == m, NEG_INF, g) for g in work]
    theta = jnp.full(s1.shape, POS_INF, F32)
    for i in range(K):
        t = None
        for bj in (bgrp if i < 2 else bgrp[:1]):
            c = jnp.where(a_rows[i] + bj >= tau, bj, POS_INF)
            t = c if t is None else jnp.minimum(t, c)
        theta_i = jnp.min(t, axis=0, keepdims=True)
        theta = jnp.where(s1 == a_rows[i], theta_i, theta)
    c1 = jnp.where(s1 >= a_rows[K - 1], jnp.exp(s1 - a_rows[0]) / zsum, 0.0)
    e2 = jnp.exp(s2 - ab_scr[K:K + 1, :])
    return theta, c1, e2


def _peer_kernel(h_ref, g_ref, sh_ref, sc_ref, gate_ref, wqT_ref, sub_ref, u_ref, vT_ref, nf_ref, o_ref,
                 xT_scr, q_scr, s_scr, e2_scr, th_scr, c1_scr, ab_scr, acc_scr, *, nheads, nkeys, final_norm):
    c, nch = pl.program_id(2), pl.num_programs(2)

    @pl.when(c == 0)
    def _prologue():
        xT = _norm_mod(h_ref, g_ref, sh_ref, sc_ref).T.astype(BF16)
        xT_scr[...] = xT
        q_scr[...] = _mm(wqT_ref[...], xT)
        dh = sub_ref.shape[2]
        for hp in range(2 * nheads):
            s_scr[hp] = _mm(sub_ref[hp], q_scr[hp * dh:(hp + 1) * dh, :].astype(BF16))

        def head_body(hh, carry):
            theta, c1, e2 = _peer_topk_head(s_scr[2 * hh], s_scr[2 * hh + 1], ab_scr)
            th_scr[hh] = theta
            c1_scr[hh] = c1
            e2_scr[hh] = e2
            return carry

        lax.fori_loop(0, nheads, head_body, 0)
        acc_scr[...] = jnp.zeros_like(acc_scr)

    act = _mm(u_ref[...], xT_scr[...])
    ge = 0.5 * act * (1.0 + lax.erf(act * (2.0 ** -0.5)))
    nsub = u_ref.shape[0] // nkeys
    parts = []
    for j in range(nsub):
        row = c * nsub + j
        g = jnp.zeros((nkeys, act.shape[1]), F32)
        for hh in range(nheads):
            thr = th_scr[hh, pl.ds(row, 1), :]
            cc = c1_scr[hh, pl.ds(row, 1), :]
            g = g + jnp.where(s_scr[2 * hh + 1] >= thr, e2_scr[hh], 0.0) * cc
        parts.append((g * ge[j * nkeys:(j + 1) * nkeys, :]).astype(BF16))
    acc_scr[...] += _mm(vT_ref[...], jnp.concatenate(parts, axis=0))

    @pl.when(c == nch - 1)
    def _epilogue():
        hn = h_ref[0] + gate_ref[0] * acc_scr[...].T
        if final_norm:
            hn = _rms(hn, nf_ref[...])
        o_ref[0] = hn


def _peer(h, g, shift, scale, gate, wqT, sub, u, vT, nf, *, final_norm):
    B, T, D = h.shape
    Tb = _tile(T, 512)
    nexp = u.shape[0]
    nkeys = sub.shape[1]
    nheads = sub.shape[0] // 2
    ec = _tile(nexp, 512)
    sh3, vmap_ = _bvec(shift)
    sc3, _ = _bvec(scale)
    g3, _ = _bvec(gate)
    row = lambda b, i, c: (b, i, 0)
    const2 = lambda b, i, c: (0, 0)
    kern = functools.partial(_peer_kernel, nheads=nheads, nkeys=nkeys, final_norm=final_norm)
    return pl.pallas_call(
        kern, grid=(B, T // Tb, nexp // ec),
        in_specs=[pl.BlockSpec((1, Tb, D), row), pl.BlockSpec(g.shape, const2),
                  pl.BlockSpec((1, 1, D), vmap_), pl.BlockSpec((1, 1, D), vmap_), pl.BlockSpec((1, 1, D), vmap_),
                  pl.BlockSpec(wqT.shape, const2), pl.BlockSpec(sub.shape, lambda b, i, c: (0, 0, 0)),
                  pl.BlockSpec((ec, D), lambda b, i, c: (c, 0)), pl.BlockSpec((D, ec), lambda b, i, c: (0, c)),
                  pl.BlockSpec(nf.shape, const2)],
        out_specs=pl.BlockSpec((1, Tb, D), row),
        out_shape=jax.ShapeDtypeStruct(h.shape, F32),
        scratch_shapes=[pltpu.VMEM((D, Tb), BF16), pltpu.VMEM((wqT.shape[0], Tb), F32),
                        pltpu.VMEM((2 * nheads, nkeys, Tb), F32), pltpu.VMEM((nheads, nkeys, Tb), F32),
                        pltpu.VMEM((nheads, nkeys, Tb), F32), pltpu.VMEM((nheads, nkeys, Tb), F32),
                        pltpu.VMEM((2 * PEER_TOPK, Tb), F32), pltpu.VMEM((D, Tb), F32)],
        compiler_params=_cparams("parallel", "parallel", "arbitrary"), name="peer")(
            h, g, sh3, sc3, g3, wqT, sub, u, vT, nf)


def _odd_in_kernel(h_ref, g_ref, sh_ref, sc_ref, win_ref, gb_ref, fm_ref, q_ref, k_ref, v_ref, og_ref, gt_ref,
                   *, qk, vd, q_scale):
    nl = _norm_mod(h_ref, g_ref, sh_ref, sc_ref)
    p = _mm(nl.astype(BF16), win_ref[...])
    q_ref[0] = (p[:, :qk] * q_scale).astype(BF16)
    k_ref[0] = p[:, qk:2 * qk].astype(BF16)
    v_ref[0] = p[:, 2 * qk:2 * qk + vd].astype(BF16)
    og_ref[0] = jax.nn.sigmoid(p[:, 2 * qk + vd:2 * qk + 2 * vd])
    gt = p[:, 2 * qk + 2 * vd:] + gb_ref[...]
    log_sig = jnp.minimum(gt, 0.0) - jnp.log1p(jnp.exp(-jnp.abs(gt)))
    gt_ref[0] = jnp.where(fm_ref[...] > 0.5, log_sig, gt)


def _odd_in(h, g, shift, scale, win, gb, fm, *, qk, vd, q_scale):
    B, T, D = h.shape
    tm = _tile(T, 512)
    sh3, vmap_ = _bvec(shift)
    sc3, _ = _bvec(scale)
    const = lambda b, i: (0, 0)
    row = lambda b, i: (b, i, 0)
    widths = (qk, qk, vd, vd, LANE)
    dts = (BF16, BF16, BF16, F32, F32)
    return pl.pallas_call(
        functools.partial(_odd_in_kernel, qk=qk, vd=vd, q_scale=q_scale), grid=(B, T // tm),
        in_specs=[pl.BlockSpec((1, tm, D), row), pl.BlockSpec(g.shape, const),
                  pl.BlockSpec((1, 1, D), vmap_), pl.BlockSpec((1, 1, D), vmap_),
                  pl.BlockSpec(win.shape, const), pl.BlockSpec(gb.shape, const), pl.BlockSpec(fm.shape, const)],
        out_specs=[pl.BlockSpec((1, tm, w), row) for w in widths],
        out_shape=[jax.ShapeDtypeStruct((B, T, w), dt) for w, dt in zip(widths, dts)],
        compiler_params=_cparams("parallel", "parallel"), name="odd_in")(h, g, sh3, sc3, win, gb, fm)


def _mlstm_chunk(qc, kc, vc, gs, gl, state, d, with_out):
    C, n, m = state
    L = qc.shape[0]
    r = lax.broadcasted_iota(jnp.int32, (L, L), 0)
    cidx = lax.broadcasted_iota(jnp.int32, (L, L), 1)
    seen = (cidx <= r) if d == 0 else (cidx >= r)
    seen_t = (r <= cidx) if d == 0 else (r >= cidx)
    ic, fc = gs[:, 2 * d:2 * d + 1], gs[:, 2 * d + 1:2 * d + 2]
    ir, fr = gl[2 * d:2 * d + 1, :], gl[2 * d + 1:2 * d + 2, :]
    hi = lax.Precision.HIGHEST
    bcm = jnp.dot(seen.astype(F32), jnp.broadcast_to(fc, (L, L)), precision=hi, preferred_element_type=F32)
    brm = jnp.dot(jnp.broadcast_to(fr, (L, L)), seen_t.astype(F32), precision=hi, preferred_element_type=F32)
    bcol, brow = bcm[:, 0:1], brm[0:1, :]
    b_end = bcol[L - 1:L, :] if d == 0 else bcol[0:1, :]
    a_row = ir + b_end - brow
    a_col = ic + b_end - bcol
    m_new = jnp.maximum(b_end + m, jnp.max(a_row, axis=1, keepdims=True))
    decay = jnp.exp(b_end + m - m_new)
    kw = kc.astype(F32) * jnp.exp(a_col - m_new)
    upd = lax.dot_general(kw.astype(BF16), vc, (((0,), (0,)), ((), ())), preferred_element_type=F32)
    new_state = (decay * C + upd, decay * n + jnp.sum(kw, axis=0, keepdims=True), m_new)
    if not with_out:
        return new_state, None
    dm = jnp.where(seen, bcm - brm + ir, NEG_INF)
    m_t = jnp.maximum(bcol + m, jnp.max(dm, axis=1, keepdims=True))
    inter = jnp.exp(bcol + m - m_t)
    sc = _mm_nt(qc, kc) * jnp.exp(dm - m_t)
    num = _mm(sc.astype(BF16), vc) + inter * _mm(qc, C.astype(BF16))
    den = jnp.sum(sc, axis=1, keepdims=True) + inter * jnp.sum(qc.astype(F32) * n, axis=1, keepdims=True)
    return new_state, num / jnp.maximum(jnp.abs(den), jnp.exp(-m_t))


def _mlstm_kernel(qc_ref, kc_ref, vc_ref, gsc_ref, glc_ref, ql_ref, kl_ref, vl_ref, gsl_ref, gll_ref, o_ref):
    L = ML_CHUNK
    ncc, ncl = gsc_ref.shape[2], gsl_ref.shape[2]
    dk, dv = ql_ref.shape[2], vl_ref.shape[2]
    for d in (0, 1):
        def ctx_body(j, st, d=d):
            cj = j if d == 0 else ncc - 1 - j
            sl = pl.ds(pl.multiple_of(cj * L, L), L)
            st, _ = _mlstm_chunk(qc_ref[0, sl, :], kc_ref[0, sl, :], vc_ref[0, sl, :],
                                 gsc_ref[0, 0, cj], glc_ref[0, 0, cj], st, d, False)
            return st

        def lat_body(j, st, d=d):
            cj = j if d == 0 else ncl - 1 - j
            sl = pl.ds(pl.multiple_of(cj * L, L), L)
            st, hout = _mlstm_chunk(ql_ref[0, sl, :], kl_ref[0, sl, :], vl_ref[0, sl, :],
                                    gsl_ref[0, 0, cj], gll_ref[0, 0, cj], st, d, True)
            if d == 0:
                o_ref[0, sl, :] = hout
            else:
                o_ref[0, sl, :] += hout
            return st

        st0 = (jnp.zeros((dk, dv), F32), jnp.zeros((1, dk), F32), jnp.zeros((1, 1), F32))
        st = lax.fori_loop(0, ncc, ctx_body, st0)
        lax.fori_loop(0, ncl, lat_body, st)


def _mlstm(qc, kc, vc, gsc, glc, ql, kl, vl, gsl, gll, *, dk, dv):
    B, T, _ = ql.shape
    Tc = qc.shape[1]
    H = ML_HEADS
    hmap = lambda b, h: (b, 0, h)
    gmap = lambda b, h: (b, h, 0, 0, 0)
    gspec = lambda a: pl.BlockSpec((1, 1) + a.shape[2:], gmap)
    return pl.pallas_call(
        _mlstm_kernel, grid=(B, H),
        in_specs=[pl.BlockSpec((1, Tc, dk), hmap), pl.BlockSpec((1, Tc, dk), hmap), pl.BlockSpec((1, Tc, dv), hmap),
                  gspec(gsc), gspec(glc),
                  pl.BlockSpec((1, T, dk), hmap), pl.BlockSpec((1, T, dk), hmap), pl.BlockSpec((1, T, dv), hmap),
                  gspec(gsl), gspec(gll)],
        out_specs=pl.BlockSpec((1, T, dv), hmap),
        out_shape=jax.ShapeDtypeStruct((B, T, H * dv), F32),
        compiler_params=_cparams("parallel", "parallel"), name="mlstm")(
            qc, kc, vc, gsc, glc, ql, kl, vl, gsl, gll)


def _gate_layouts(gt, nc):
    B = gt.shape[0]
    g = gt[:, :, :4 * ML_HEADS].reshape(B, nc, ML_CHUNK, 4, ML_HEADS)
    return g.transpose(0, 4, 1, 2, 3), g.transpose(0, 4, 1, 3, 2)


def _odd_out_kernel(hs_ref, og_ref, hg_ref, w_ref, h_ref, gate_ref, o_ref, *, dv):
    hs = hs_ref[0]
    parts = [hs[:, j * dv:(j + 1) * dv] for j in range(hs.shape[1] // dv)]
    normed = jnp.concatenate(
        [p * lax.rsqrt(jnp.mean(p * p, axis=-1, keepdims=True) + EPS) for p in parts], axis=1) * hg_ref[...]
    o_ref[0] = h_ref[0] + gate_ref[0] * _mm((og_ref[0] * normed).astype(BF16), w_ref[...])


def _odd_out(hs, og, hg, w, h, gate, *, dv):
    B, T, D = h.shape
    tm = _tile(T, 512)
    mix = hs.shape[-1]
    g3, gmap = _bvec(gate)
    row = lambda b, i: (b, i, 0)
    const = lambda b, i: (0, 0)
    return pl.pallas_call(
        functools.partial(_odd_out_kernel, dv=dv), grid=(B, T // tm),
        in_specs=[pl.BlockSpec((1, tm, mix), row), pl.BlockSpec((1, tm, mix), row), pl.BlockSpec(hg.shape, const),
                  pl.BlockSpec(w.shape, const), pl.BlockSpec((1, tm, D), row), pl.BlockSpec((1, 1, D), gmap)],
        out_specs=pl.BlockSpec((1, tm, D), row),
        out_shape=jax.ShapeDtypeStruct(h.shape, F32),
        compiler_params=_cparams("parallel", "parallel"), name="odd_out")(hs, og, hg, w, h, g3)


def _rope_tables(seq):
    n_freq = MLA_ROPE // 4
    inv = ROPE_BASE ** (-jnp.arange(n_freq, dtype=F32) / n_freq)
    t = jnp.arange(seq, dtype=jnp.int32)
    row = (t // GRID_W).astype(F32)
    col = (t % GRID_W).astype(F32)
    ang = jnp.concatenate([row[:, None] * inv, col[:, None] * inv], axis=-1)
    ones = jnp.ones((seq, MLA_NOPE), F32)
    tail = jnp.ones((seq, LANE - MLA_NOPE - MLA_ROPE), F32)
    cos = jnp.concatenate([ones, jnp.cos(ang), jnp.cos(ang), tail], axis=-1)
    sin = jnp.concatenate([0 * ones, jnp.sin(ang), jnp.sin(ang), 0 * tail], axis=-1)
    return cos, sin


def _rot_cols(w):
    half = w.shape[-1] // 2
    return jnp.concatenate([-w[..., half:], w[..., :half]], axis=-1)


def _slot(cols, width=LANE):
    pad = width - cols.shape[-1]
    out = jnp.pad(cols, [(0, 0)] * (cols.ndim - 1) + [(0, pad)])
    return out.reshape(out.shape[:-2] + (out.shape[-2] * width,))


def _even_weights(w_in, w_uq, w_ukv, w_out, q_rank, kv_rank):
    D = w_in.shape[0]
    H = MLA_HEADS
    o = q_rank + kv_rank
    w_kr = w_in[:, o:o + MLA_ROPE]
    lead = jnp.zeros((D, MLA_NOPE), F32)
    tail = jnp.zeros((D, LANE - MLA_NOPE - MLA_ROPE), F32)
    win = jnp.concatenate([w_in[:, :o], lead, w_kr, tail, lead, _rot_cols(w_kr), tail, w_in[:, o + MLA_ROPE:]],
                          axis=1).astype(BF16)
    uq = w_uq.reshape(q_rank, H, MLA_NOPE + MLA_ROPE)
    uq_rot = jnp.concatenate([jnp.zeros_like(uq[..., :MLA_NOPE]), _rot_cols(uq[..., MLA_NOPE:])], axis=-1)
    wq2 = jnp.concatenate([_slot(uq), _slot(uq_rot)], axis=1).astype(BF16)
    ukv = w_ukv.reshape(kv_rank, H, MLA_NOPE + MLA_V)
    wkv2 = jnp.concatenate([_slot(ukv[..., :MLA_NOPE]), _slot(ukv[..., MLA_NOPE:])], axis=1).astype(BF16)
    nattn = H * MLA_V
    wa = jnp.pad(w_out[:nattn].reshape(H, MLA_V, D), ((0, 0), (0, LANE - MLA_V), (0, 0))).reshape(H * LANE, D)
    return win, wq2, wkv2, wa.astype(BF16), w_out[nattn:].astype(BF16)


def kernel(x, c, ctx, c_ctx, norm1_g, norm2_g, w_mod, b_mod, even_w_in, mla_q_norm, mla_kv_norm, mla_w_uq, mla_w_ukv, conv_w, even_w_out, odd_w_in, mlstm_gate_b, mlstm_head_g, odd_w_out, peer_w_q, peer_subkeys, peer_u, peer_v, norm_f_g):
    B, S, D = x.shape
    n_ctx = ctx.shape[1]
    depth = norm1_g.shape[0]
    hl, hc = x, ctx
    row2 = lambda v: v[None, :]

    cc = jnp.zeros((-(-(B + 1) // 8) * 8, D), F32).at[:B].set(c).at[B].set(c_ctx)
    cos_l, sin_l = _rope_tables(S)
    cos_c = jnp.ones((n_ctx, LANE), F32)
    sin_c = jnp.zeros((n_ctx, LANE), F32)

    for i in range(depth):
        last = i == depth - 1
        j = i // 2
        mod = _mod_vectors(cc, w_mod[i].astype(BF16), row2(b_mod[i]))
        mod_l = [mod[:B, k * D:(k + 1) * D] for k in range(6)]
        mod_c = [mod[B:B + 1, k * D:(k + 1) * D] for k in range(6)]
        g1, g2 = row2(norm1_g[i]), row2(norm2_g[i])

        if i % 2 == 0:
            q_rank, kv_rank = mla_q_norm.shape[1], mla_kv_norm.shape[1]
            conv_dim = conv_w.shape[-1]
            win, wq2, wkv2, wa, wc = _even_weights(even_w_in[j], mla_w_uq[j], mla_w_ukv[j], even_w_out[j],
                                                   q_rank, kv_rank)
            qn, kvn = row2(mla_q_norm[j]), row2(mla_kv_norm[j])
            kw = dict(conv_dim=conv_dim, q_scale=float((MLA_NOPE + MLA_ROPE) ** -0.5))
            ql, kl, vl, bgl, zl = _even_in(hl, g1, mod_l[0], mod_l[1], win, qn, kvn, wq2, wkv2, cos_l, sin_l, **kw)
            qc, kc, vc, bgc, zc = _even_in(hc, g1, mod_c[0], mod_c[1], win, qn, kvn, wq2, wkv2, cos_c, sin_c, **kw)
            ol = _attention(ql, [kl, kc], [vl, vc])
            hl = _even_out(ol, bgl, zl, hl, mod_l[2], conv_w[j], wa, wc)
            if not last:
                oc = _attention(qc, [kc], [vc])
                hc = _even_out(oc, bgc, zc, hc, mod_c[2], conv_w[j], wa, wc)
        else:
            H = ML_HEADS
            mix = mlstm_head_g.shape[1]
            dv = mix // H
            qkvg = odd_w_in.shape[2] - mix
            dk = (qkvg - 4 * H - mix) // (2 * H)
            qk = H * dk
            w = odd_w_in[j]
            win = jnp.concatenate([w[:, :2 * qk + mix], w[:, qkvg:], w[:, 2 * qk + mix:qkvg],
                                   jnp.zeros((D, LANE - 4 * H), F32)], axis=1).astype(BF16)
            gb = jnp.pad(mlstm_gate_b[j], (0, LANE - 4 * H))[None, :]
            fm = jnp.pad(jnp.tile(jnp.repeat(jnp.array([0.0, 1.0], F32), H), 2), (0, LANE - 4 * H))[None, :]
            kw = dict(qk=qk, vd=mix, q_scale=float(dk ** -0.5))
            ql, kl, vl, ogl, gtl = _odd_in(hl, g1, mod_l[0], mod_l[1], win, gb, fm, **kw)
            qc, kc, vc, ogc, gtc = _odd_in(hc, g1, mod_c[0], mod_c[1], win, gb, fm, **kw)
            gsl, gll = _gate_layouts(gtl, S // ML_CHUNK)
            gsc, glc = _gate_layouts(gtc, n_ctx // ML_CHUNK)
            hs = _mlstm(qc, kc, vc, gsc, glc, ql, kl, vl, gsl, gll, dk=dk, dv=dv)
            hg, wo = row2(mlstm_head_g[j]), odd_w_out[j].astype(BF16)
            hl = _odd_out(hs, ogl, hg, wo, hl, mod_l[2], dv=dv)
            if not last:
                raise NotImplementedError("context readout of an mLSTM layer is only needed when a layer follows it")

        nheads = peer_subkeys.shape[1]
        wqT = peer_w_q[i].T.astype(BF16)
        sub = peer_subkeys[i].reshape((2 * nheads,) + peer_subkeys.shape[3:]).astype(BF16)
        u, vT = peer_u[i].astype(BF16), peer_v[i].T.astype(BF16)
        nf = row2(norm_f_g)
        hl = _peer(hl, g2, mod_l[3], mod_l[4], mod_l[5], wqT, sub, u, vT, nf, final_norm=last)
        if not last:
            hc = _peer(hc, g2, mod_c[3], mod_c[4], mod_c[5], wqT, sub, u, vT, nf, final_norm=False)
    return hl
```

```python
import functools

import jax
import jax.numpy as jnp
from jax import lax
from jax.experimental import pallas as pl
from jax.experimental.pallas import tpu as pltpu

GRID_W = 64
EPS = 1e-6
ROPE_BASE = 10000.0
MLA_HEADS = 8
MLA_NOPE = 64
MLA_ROPE = 32
MLA_V = 64
ML_HEADS = 4
ML_CHUNK = 64
PEER_TOPK = 16

F32 = jnp.float32
BF16 = jnp.bfloat16
LANE = 128
VMEM_LIMIT = 56 * 1024 * 1024
NEG_INF = float("-inf")


def _cparams(*sem):
    return pltpu.CompilerParams(dimension_semantics=sem, vmem_limit_bytes=VMEM_LIMIT)


def _tile(n, pref):
    t = min(n, pref)
    assert n % t == 0, (n, pref)
    return t


def _rms(x, g):
    return x * lax.rsqrt(jnp.mean(x * x, axis=-1, keepdims=True) + EPS) * g


def _norm_mod(h_ref, g_ref, sh_ref, sc_ref):
    return _rms(h_ref[0], g_ref[...]) * (1.0 + sc_ref[0]) + sh_ref[0]


def _mm(a, b):
    return jnp.dot(a, b, preferred_element_type=F32)


def _mm_nt(a, b):
    return lax.dot_general(a, b, (((1,), (1,)), ((), ())), preferred_element_type=F32)


def _mod_kernel(c_ref, w_ref, b_ref, o_ref):
    c = c_ref[...]
    o_ref[...] = _mm((c * jax.nn.sigmoid(c)).astype(BF16), w_ref[...]) + b_ref[...]


def _mod_vectors(cc, w, b):
    R, D = cc.shape
    N = w.shape[1]
    tn = _tile(N, 1024)
    return pl.pallas_call(
        _mod_kernel, grid=(N // tn,),
        in_specs=[pl.BlockSpec((R, D), lambda j: (0, 0)),
                  pl.BlockSpec((D, tn), lambda j: (0, j)),
                  pl.BlockSpec((1, tn), lambda j: (0, j))],
        out_specs=pl.BlockSpec((R, tn), lambda j: (0, j)),
        out_shape=jax.ShapeDtypeStruct((R, N), F32),
        compiler_params=_cparams("parallel"), name="mod_vectors")(cc, w, b)


def _bvec(v):
    v3 = v[:, None, :]
    if v3.shape[0] == 1:
        return v3, (lambda b, *_: (0, 0, 0))
    return v3, (lambda b, *_: (b, 0, 0))


def _even_in_kernel(h_ref, g_ref, sh_ref, sc_ref, win_ref, qn_ref, kvn_ref, wq2_ref, wkv2_ref, cos_ref, sin_ref,
                    q_ref, k_ref, v_ref, bg_ref, z_ref, *, q_rank, kv_rank, conv_dim, q_scale):
    nl = _norm_mod(h_ref, g_ref, sh_ref, sc_ref)
    p = _mm(nl.astype(BF16), win_ref[...])
    o = q_rank + kv_rank
    pq, pc = p[:, :q_rank], p[:, q_rank:o]
    kr, krr = p[:, o:o + LANE], p[:, o + LANE:o + 2 * LANE]
    o += 2 * LANE
    bg, cg, u = p[:, o:o + conv_dim], p[:, o + conv_dim:o + 2 * conv_dim], p[:, o + 2 * conv_dim:o + 3 * conv_dim]
    cos, sin = cos_ref[...], sin_ref[...]
    hw = MLA_HEADS * LANE
    cos_t, sin_t = jnp.tile(cos, (1, MLA_HEADS)), jnp.tile(sin, (1, MLA_HEADS))
    qq = _mm(_rms(pq, qn_ref[...]).astype(BF16), wq2_ref[...])
    q_ref[0] = ((qq[:, :hw] * cos_t + qq[:, hw:] * sin_t) * q_scale).astype(BF16)
    kv = _mm(_rms(pc, kvn_ref[...]).astype(BF16), wkv2_ref[...])
    k_ref[0] = (kv[:, :hw] + jnp.tile(kr * cos + krr * sin, (1, MLA_HEADS))).astype(BF16)
    v_ref[0] = kv[:, hw:].astype(BF16)
    bg_ref[0] = bg
    z_ref[0] = cg * u


def _even_in(h, g, shift, scale, win, qn, kvn, wq2, wkv2, cos, sin, *, conv_dim, q_scale):
    B, T, D = h.shape
    tm = _tile(T, 512)
    hw = MLA_HEADS * LANE
    sh3, vmap_ = _bvec(shift)
    sc3, _ = _bvec(scale)
    const = lambda b, i: (0, 0)
    row = lambda b, i: (b, i, 0)
    kern = functools.partial(_even_in_kernel, q_rank=qn.shape[1], kv_rank=kvn.shape[1], conv_dim=conv_dim,
                             q_scale=q_scale)
    return pl.pallas_call(
        kern, grid=(B, T // tm),
        in_specs=[pl.BlockSpec((1, tm, D), row), pl.BlockSpec(g.shape, const),
                  pl.BlockSpec((1, 1, D), vmap_), pl.BlockSpec((1, 1, D), vmap_),
                  pl.BlockSpec(win.shape, const), pl.BlockSpec(qn.shape, const), pl.BlockSpec(kvn.shape, const),
                  pl.BlockSpec(wq2.shape, const), pl.BlockSpec(wkv2.shape, const),
                  pl.BlockSpec((tm, LANE), lambda b, i: (i, 0)), pl.BlockSpec((tm, LANE), lambda b, i: (i, 0))],
        out_specs=[pl.BlockSpec((1, tm, hw), row)] * 3 + [pl.BlockSpec((1, tm, conv_dim), row)] * 2,
        out_shape=[jax.ShapeDtypeStruct((B, T, hw), BF16)] * 3 + [jax.ShapeDtypeStruct((B, T, conv_dim), F32)] * 2,
        compiler_params=_cparams("parallel", "parallel"), name="even_in")(
            h, g, sh3, sc3, win, qn, kvn, wq2, wkv2, cos, sin)


def _attn_kernel(q_ref, *refs, nsets, hp):
    k_refs, v_refs, o_ref = refs[:nsets], refs[nsets:2 * nsets], refs[2 * nsets]
    for p in range(hp):
        lanes = slice(p * LANE, (p + 1) * LANE)
        q = q_ref[0, :, lanes]
        ss = [_mm_nt(q, k[0, :, lanes]) for k in k_refs]
        m = functools.reduce(jnp.maximum, [jnp.max(s, axis=-1, keepdims=True) for s in ss])
        ps = [jnp.exp(s - m) for s in ss]
        l = functools.reduce(jnp.add, [jnp.sum(pr, axis=-1, keepdims=True) for pr in ps])
        o = functools.reduce(jnp.add, [_mm(pr.astype(BF16), v[0, :, lanes]) for pr, v in zip(ps, v_refs)])
        o_ref[0, :, lanes] = (o / l).astype(BF16)


def _attention(q, ks, vs):
    B, T, _ = q.shape
    tq = _tile(T, 256)
    hp = 2
    qmap = lambda b, h, i: (b, i, h)
    kmap = lambda b, h, i: (b, 0, h)
    kspecs = [pl.BlockSpec((1, k.shape[1], hp * LANE), kmap) for k in ks]
    return pl.pallas_call(
        functools.partial(_attn_kernel, nsets=len(ks), hp=hp), grid=(B, MLA_HEADS // hp, T // tq),
        in_specs=[pl.BlockSpec((1, tq, hp * LANE), qmap)] + kspecs + kspecs,
        out_specs=pl.BlockSpec((1, tq, hp * LANE), qmap),
        out_shape=jax.ShapeDtypeStruct(q.shape, BF16),
        compiler_params=_cparams("parallel", "parallel", "parallel"), name="mla_attention")(q, *ks, *vs)


def _even_out_kernel(a_ref, bg_ref, z_ref, zp_ref, zn_ref, h_ref, gate_ref, cw_ref, wa_ref, wc_ref, o_ref):
    i, last = pl.program_id(1), pl.num_programs(1) - 1
    z = z_ref[0]
    tm = z.shape[0]
    rows = lax.broadcasted_iota(jnp.int32, z.shape, 0)
    prev_row = zp_ref[0][7:8, :] * (i > 0).astype(F32)
    next_row = zn_ref[0][0:1, :] * (i < last).astype(F32)
    zm1 = jnp.where(rows == 0, prev_row, pltpu.roll(z, 1, 0))
    zp1 = jnp.where(rows == tm - 1, next_row, pltpu.roll(z, tm - 1, 0))
    cw = cw_ref[...]
    y = cw[0:1, :] * zm1 + cw[1:2, :] * z + cw[2:3, :] * zp1
    o = _mm(a_ref[0], wa_ref[...]) + _mm((bg_ref[0] * y).astype(BF16), wc_ref[...])
    o_ref[0] = h_ref[0] + gate_ref[0] * o


def _even_out(attn, bg, z, h, gate, cw, wa, wc):
    B, T, D = h.shape
    tm = _tile(T, 512)
    cd = z.shape[-1]
    nb8 = T // 8
    g3, gmap = _bvec(gate)
    row = lambda b, i: (b, i, 0)
    const = lambda b, i: (0, 0)
    return pl.pallas_call(
        _even_out_kernel, grid=(B, T // tm),
        in_specs=[pl.BlockSpec((1, tm, attn.shape[-1]), row), pl.BlockSpec((1, tm, cd), row),
                  pl.BlockSpec((1, tm, cd), row),
                  pl.BlockSpec((1, 8, cd), lambda b, i: (b, jnp.maximum(i * (tm // 8) - 1, 0), 0)),
                  pl.BlockSpec((1, 8, cd), lambda b, i: (b, jnp.minimum((i + 1) * (tm // 8), nb8 - 1), 0)),
                  pl.BlockSpec((1, tm, D), row), pl.BlockSpec((1, 1, D), gmap),
                  pl.BlockSpec(cw.shape, const), pl.BlockSpec(wa.shape, const), pl.BlockSpec(wc.shape, const)],
        out_specs=pl.BlockSpec((1, tm, D), row),
        out_shape=jax.ShapeDtypeStruct(h.shape, F32),
        compiler_params=_cparams("parallel", "parallel"), name="even_out")(
            attn, bg, z, z, z, h, g3, cw, wa, wc)


def _peer_topk_head(s1, s2, ab_scr):
    K = PEER_TOPK
    rank2 = jnp.full(s2.shape, float(K), F32)
    for base, s in ((0, s1), (K, s2)):
        work = s
        for r in range(K):
            m = jnp.max(work, axis=0, keepdims=True)
            ab_scr[base + r:base + r + 1, :] = m
            hit = work == m
            if base:
                rank2 = jnp.where(hit, float(r), rank2)
            if r + 1 < K:
                work = jnp.where(hit, NEG_INF, work)
    bgrp = [ab_scr[K:K + 8, :], ab_scr[K + 8:K + 16, :]]
    a_rows = [ab_scr[i:i + 1, :] for i in range(K)]
    groups = []
    for i in range(K):
        for bj in (bgrp if i < 2 else bgrp[:1]):
            groups.append(a_rows[i] + bj)
    top = a_rows[0] + ab_scr[K:K + 1, :]
    zsum = jnp.zeros_like(top)
    tau = top
    work = groups
    for r in range(K):
        m = jnp.max(functools.reduce(jnp.maximum, work), axis=0, keepdims=True)
        zsum = zsum + jnp.exp(m - top)
        tau = m
        if r + 1 < K:
            work = [jnp.where(g == m, NEG_INF, g) for g in work]
    cnt1 = jnp.zeros(s1.shape, F32)
    for i in range(K):
        t = None
        for bj in (bgrp if i < 2 else bgrp[:1]):
            c = jnp.where(a_rows[i] + bj >= tau, 1.0, 0.0)
            t = c if t is None else t + c
        cnt1 = jnp.where(s1 == a_rows[i], jnp.sum(t, axis=0, keepdims=True), cnt1)
    c1 = jnp.where(s1 >= a_rows[K - 1], jnp.exp(s1 - a_rows[0]) / zsum, 0.0)
    e2 = jnp.exp(s2 - ab_scr[K:K + 1, :])
    return rank2, cnt1, c1, e2


def _peer_route_kernel(h_ref, g_ref, sh_ref, sc_ref, wqT_ref, sub_ref, xT_ref, r2_ref, e2_ref, n1_ref, c1_ref,
                       q_scr, s_scr, ab_scr, *, nheads):
    xT = _norm_mod(h_ref, g_ref, sh_ref, sc_ref).T.astype(BF16)
    xT_ref[0, 0] = xT
    q_scr[...] = _mm(wqT_ref[...], xT)
    dh = sub_ref.shape[2]
    for hp in range(2 * nheads):
        s_scr[hp] = _mm(sub_ref[hp], q_scr[hp * dh:(hp + 1) * dh, :].astype(BF16))

    def head_body(hh, carry):
        rank2, cnt1, c1, e2 = _peer_topk_head(s_scr[2 * hh], s_scr[2 * hh + 1], ab_scr)
        r2_ref[0, 0, hh] = rank2.astype(BF16)
        e2_ref[0, 0, hh] = e2.astype(BF16)
        n1_ref[0, 0, hh] = cnt1
        c1_ref[0, 0, hh] = c1
        return carry

    lax.fori_loop(0, nheads, head_body, 0)


def _peer_gate_tiles(key0, nsub, g_scr, r2_ref, e2_ref, n1_ref, c1_ref, *, nheads, nkeys, sub_rows=16):
    tb = g_scr.shape[1]
    row_bf16 = lambda ref, hh, j: jnp.broadcast_to(ref[0, 0, hh, key0 + j:key0 + j + 1, :],
                                                   (sub_rows, tb)).astype(BF16)
    cnt = [[row_bf16(n1_ref, hh, j) for j in range(nsub)] for hh in range(nheads)]
    cc = [[row_bf16(c1_ref, hh, j) for j in range(nsub)] for hh in range(nheads)]
    zero = jnp.zeros((sub_rows, tb), BF16)
    for rg in range(nkeys // sub_rows):
        rows = slice(rg * sub_rows, (rg + 1) * sub_rows)
        accs = [None] * nsub
        for hh in range(nheads):
            r2t, e2t = r2_ref[0, 0, hh, rows, :], e2_ref[0, 0, hh, rows, :]
            for j in range(nsub):
                t = jnp.where(r2t < cnt[hh][j], e2t, zero) * cc[hh][j]
                accs[j] = t if accs[j] is None else accs[j] + t
        for j in range(nsub):
            lo = (key0 + j) * nkeys + rg * sub_rows
            g_scr[lo:lo + sub_rows, :] = accs[j]


def _peer_expert_kernel(h_ref, gate_ref, nf_ref, xT_ref, r2_ref, e2_ref, n1_ref, c1_ref, u_ref, vTp_ref, vTl_ref,
                        o_ref, acc_scr, g_scr, w_scr, *, nheads, nkeys, final_norm):
    c, nch = pl.program_id(2), pl.num_programs(2)

    @pl.when(c == 0)
    def _init():
        acc_scr[...] = jnp.zeros_like(acc_scr)
        w_scr[...] = jnp.zeros_like(w_scr)

    nsub = u_ref.shape[0] // nkeys
    group = min(nsub, 4)
    for k0 in range(0, nsub, group):
        _peer_gate_tiles(k0, group, g_scr, r2_ref, e2_ref, n1_ref, c1_ref, nheads=nheads, nkeys=nkeys)
    acc_scr[...] += _mm(vTp_ref[...], w_scr[...])
    act = _mm(u_ref[...], xT_ref[0, 0])
    ge = 0.5 * act * (1.0 + lax.erf(act * (2.0 ** -0.5)))
    w_scr[...] = g_scr[...] * ge.astype(BF16)

    @pl.when(c == nch - 1)
    def _epilogue():
        acc = acc_scr[...] + _mm(vTl_ref[...], w_scr[...])
        hn = h_ref[0] + gate_ref[0] * acc.T
        if final_norm:
            hn = _rms(hn, nf_ref[...])
        o_ref[0] = hn


def _peer(h, g, shift, scale, gate, wqT, sub, u, vT, nf, *, final_norm):
    B, T, D = h.shape
    Tb = _tile(T, 512)
    nexp = u.shape[0]
    nkeys = sub.shape[1]
    nheads = sub.shape[0] // 2
    ec = _tile(nexp, 1024)
    nst, nblk, nsub = nexp // ec, T // Tb, ec // nkeys
    sh3, vmap2 = _bvec(shift)
    sc3, _ = _bvec(scale)
    g3, vmap3 = _bvec(gate)

    blk2 = lambda b, i: (b, i, 0, 0, 0)
    hshape = (B, nblk, nheads, nkeys, Tb)
    hspec2 = pl.BlockSpec((1, 1, nheads, nkeys, Tb), blk2)
    xT, r2, e2, n1, c1 = pl.pallas_call(
        functools.partial(_peer_route_kernel, nheads=nheads), grid=(B, nblk),
        in_specs=[pl.BlockSpec((1, Tb, D), lambda b, i: (b, i, 0)), pl.BlockSpec(g.shape, lambda b, i: (0, 0)),
                  pl.BlockSpec((1, 1, D), vmap2), pl.BlockSpec((1, 1, D), vmap2),
                  pl.BlockSpec(wqT.shape, lambda b, i: (0, 0)), pl.BlockSpec(sub.shape, lambda b, i: (0, 0, 0))],
        out_specs=[pl.BlockSpec((1, 1, D, Tb), lambda b, i: (b, i, 0, 0))] + [hspec2] * 4,
        out_shape=[jax.ShapeDtypeStruct((B, nblk, D, Tb), BF16)]
        + [jax.ShapeDtypeStruct(hshape, dt) for dt in (BF16, BF16, F32, F32)],
        scratch_shapes=[pltpu.VMEM((wqT.shape[0], Tb), F32), pltpu.VMEM((2 * nheads, nkeys, Tb), F32),
                        pltpu.VMEM((2 * PEER_TOPK, Tb), F32)],
        compiler_params=_cparams("parallel", "parallel"), name="peer_route")(h, g, sh3, sc3, wqT, sub)

    row = lambda b, i, c: (b, i, 0)
    const2 = lambda b, i, c: (0, 0)
    blk3 = lambda b, i, c: (b, i, 0, 0, 0)
    hspec3 = pl.BlockSpec((1, 1, nheads, nkeys, Tb), blk3)
    cspec3 = pl.BlockSpec((1, 1, nheads, nsub, Tb), lambda b, i, c: (b, i, 0, c, 0))
    kern = functools.partial(_peer_expert_kernel, nheads=nheads, nkeys=nkeys, final_norm=final_norm)
    return pl.pallas_call(
        kern, grid=(B, nblk, nst),
        in_specs=[pl.BlockSpec((1, Tb, D), row), pl.BlockSpec((1, 1, D), vmap3), pl.BlockSpec(nf.shape, const2),
                  pl.BlockSpec((1, 1, D, Tb), lambda b, i, c: (b, i, 0, 0)), hspec3, hspec3, cspec3, cspec3,
                  pl.BlockSpec((ec, D), lambda b, i, c: (c, 0)),
                  pl.BlockSpec((D, ec), lambda b, i, c: (0, jnp.maximum(c - 1, 0))),
                  pl.BlockSpec((D, ec), lambda b, i, c: (0, jnp.where(c == nst - 1, nst - 1, 0)))],
        out_specs=pl.BlockSpec((1, Tb, D), row),
        out_shape=jax.ShapeDtypeStruct(h.shape, F32),
        scratch_shapes=[pltpu.VMEM((D, Tb), F32), pltpu.VMEM((ec, Tb), BF16), pltpu.VMEM((ec, Tb), BF16)],
        compiler_params=_cparams("parallel", "parallel", "arbitrary"), name="peer_experts")(
            h, g3, nf, xT, r2, e2, n1, c1, u, vT, vT)


def _odd_in_kernel(h_ref, g_ref, sh_ref, sc_ref, win_ref, gb_ref, fm_ref, q_ref, k_ref, v_ref, og_ref, gt_ref,
                   *, qk, vd, q_scale):
    nl = _norm_mod(h_ref, g_ref, sh_ref, sc_ref)
    p = _mm(nl.astype(BF16), win_ref[...])
    q_ref[0] = (p[:, :qk] * q_scale).astype(BF16)
    k_ref[0] = p[:, qk:2 * qk].astype(BF16)
    v_ref[0] = p[:, 2 * qk:2 * qk + vd].astype(BF16)
    og_ref[0] = jax.nn.sigmoid(p[:, 2 * qk + vd:2 * qk + 2 * vd])
    gt = p[:, 2 * qk + 2 * vd:] + gb_ref[...]
    log_sig = jnp.minimum(gt, 0.0) - jnp.log1p(jnp.exp(-jnp.abs(gt)))
    gt_ref[0] = jnp.where(fm_ref[...] > 0.5, log_sig, gt)


def _odd_in(h, g, shift, scale, win, gb, fm, *, qk, vd, q_scale):
    B, T, D = h.shape
    tm = _tile(T, 512)
    sh3, vmap_ = _bvec(shift)
    sc3, _ = _bvec(scale)
    const = lambda b, i: (0, 0)
    row = lambda b, i: (b, i, 0)
    widths = (qk, qk, vd, vd, LANE)
    dts = (BF16, BF16, BF16, F32, F32)
    return pl.pallas_call(
        functools.partial(_odd_in_kernel, qk=qk, vd=vd, q_scale=q_scale), grid=(B, T // tm),
        in_specs=[pl.BlockSpec((1, tm, D), row), pl.BlockSpec(g.shape, const),
                  pl.BlockSpec((1, 1, D), vmap_), pl.BlockSpec((1, 1, D), vmap_),
                  pl.BlockSpec(win.shape, const), pl.BlockSpec(gb.shape, const), pl.BlockSpec(fm.shape, const)],
        out_specs=[pl.BlockSpec((1, tm, w), row) for w in widths],
        out_shape=[jax.ShapeDtypeStruct((B, T, w), dt) for w, dt in zip(widths, dts)],
        compiler_params=_cparams("parallel", "parallel"), name="odd_in")(h, g, sh3, sc3, win, gb, fm)


def _mlstm_chunks(qs, ks, vs, gls, states, dirs, with_out):
    G = range(len(dirs))
    L = qs[0].shape[0]
    r = lax.broadcasted_iota(jnp.int32, (L, L), 0)
    cidx = lax.broadcasted_iota(jnp.int32, (L, L), 1)
    eye = (r == cidx).astype(F32)
    seen = {0: cidx <= r, 1: cidx >= r}
    seen_f = {d: seen[d].astype(F32) for d in set(dirs)}
    seen_t = {d: seen[1 - d].astype(F32) for d in set(dirs)}
    hi = lax.Precision.HIGHEST
    nt = (((1,), (1,)), ((), ()))
    Cs, ns, ms = zip(*states)
    ir = [gls[g][2 * dirs[g]:2 * dirs[g] + 1, :] for g in G]
    frm = [jnp.broadcast_to(gls[g][2 * dirs[g] + 1:2 * dirs[g] + 2, :], (L, L)) for g in G]
    irm = [jnp.broadcast_to(ir[g], (L, L)) for g in G]
    bcm = [lax.dot_general(seen_f[dirs[g]], frm[g], nt, precision=hi, preferred_element_type=F32) for g in G]
    brm = [jnp.dot(frm[g], seen_t[dirs[g]], precision=hi, preferred_element_type=F32) for g in G]
    icm = [lax.dot_general(eye, irm[g], nt, precision=hi, preferred_element_type=F32) for g in G]
    bcol = [bcm[g][:, 0:1] for g in G]
    brow = [brm[g][0:1, :] for g in G]
    b_end = [bcol[g][L - 1:L, :] if dirs[g] == 0 else bcol[g][0:1, :] for g in G]
    a_row = [ir[g] + b_end[g] - brow[g] for g in G]
    a_col = [icm[g][:, 0:1] + b_end[g] - bcol[g] for g in G]
    m_new = [jnp.maximum(b_end[g] + ms[g], jnp.max(a_row[g], axis=1, keepdims=True)) for g in G]
    decay = [jnp.exp(b_end[g] + ms[g] - m_new[g]) for g in G]
    kw = [ks[g].astype(F32) * jnp.exp(a_col[g] - m_new[g]) for g in G]
    upd = [lax.dot_general(kw[g].astype(BF16), vs[g], (((0,), (0,)), ((), ())), preferred_element_type=F32)
           for g in G]
    new_states = [(decay[g] * Cs[g] + upd[g], decay[g] * ns[g] + jnp.sum(kw[g], axis=0, keepdims=True), m_new[g])
                  for g in G]
    if not with_out:
        return new_states, None
    dm = [jnp.where(seen[dirs[g]], bcm[g] - brm[g] + ir[g], NEG_INF) for g in G]
    m_t = [jnp.maximum(bcol[g] + ms[g], jnp.max(dm[g], axis=1, keepdims=True)) for g in G]
    inter = [jnp.exp(bcol[g] + ms[g] - m_t[g]) for g in G]
    sc = [_mm_nt(qs[g], ks[g]) * jnp.exp(dm[g] - m_t[g]) for g in G]
    intra = [_mm(sc[g].astype(BF16), vs[g]) for g in G]
    cross = [_mm(qs[g], Cs[g].astype(BF16)) for g in G]
    den = [jnp.sum(sc[g], axis=1, keepdims=True)
           + inter[g] * jnp.sum(qs[g].astype(F32) * ns[g], axis=1, keepdims=True) for g in G]
    outs = [(intra[g] + inter[g] * cross[g]) / jnp.maximum(jnp.abs(den[g]), jnp.exp(-m_t[g])) for g in G]
    return new_states, outs


def _mlstm_kernel(qc_ref, kc_ref, vc_ref, glc_ref, ql_ref, kl_ref, vl_ref, gll_ref, o_ref, *, hp, dk, dv):
    L = ML_CHUNK
    ncc, ncl = glc_ref.shape[2], gll_ref.shape[2]
    chains = [(p, d) for p in range(hp) for d in (0, 1)]

    def scan(q_ref, k_ref, v_ref, gl_ref, nc, lo, hi, states, out_mode):
        def body(j, sts):
            cjs = [j if d == 0 else nc - 1 - j for _, d in chains]
            sls = [pl.ds(pl.multiple_of(cj * L, L), L) for cj in cjs]
            qs = [q_ref[0, sl, p * dk:(p + 1) * dk] for (p, _), sl in zip(chains, sls)]
            ks = [k_ref[0, sl, p * dk:(p + 1) * dk] for (p, _), sl in zip(chains, sls)]
            vs = [v_ref[0, sl, p * dv:(p + 1) * dv] for (p, _), sl in zip(chains, sls)]
            gls = [gl_ref[0, p, cj] for (p, _), cj in zip(chains, cjs)]
            prev = [o_ref[0, sl, p * dv:(p + 1) * dv] for (p, _), sl in zip(chains, sls)] if out_mode == "add" else None
            new, outs = _mlstm_chunks(qs, ks, vs, gls, sts, [d for _, d in chains], out_mode is not None)
            if outs is not None:
                for g, ((p, _), sl) in enumerate(zip(chains, sls)):
                    o_ref[0, sl, p * dv:(p + 1) * dv] = outs[g] if prev is None else prev[g] + outs[g]
            return tuple(new)

        return lax.fori_loop(lo, hi, body, states)

    assert ncl % 2 == 0, ncl
    st = tuple((jnp.zeros((dk, dv), F32), jnp.zeros((1, dk), F32), jnp.zeros((1, 1), F32)) for _ in chains)
    st = scan(qc_ref, kc_ref, vc_ref, glc_ref, ncc, 0, ncc, st, None)
    st = scan(ql_ref, kl_ref, vl_ref, gll_ref, ncl, 0, ncl // 2, st, "assign")
    scan(ql_ref, kl_ref, vl_ref, gll_ref, ncl, ncl // 2, ncl, st, "add")


def _mlstm(qc, kc, vc, glc, ql, kl, vl, gll, *, dk, dv):
    B, T, _ = ql.shape
    Tc = qc.shape[1]
    H = ML_HEADS
    hp = 2
    hmap = lambda b, h: (b, 0, h)
    gspec = lambda a: pl.BlockSpec((1, hp) + a.shape[2:], lambda b, h: (b, h, 0, 0, 0))
    return pl.pallas_call(
        functools.partial(_mlstm_kernel, hp=hp, dk=dk, dv=dv), grid=(B, H // hp),
        in_specs=[pl.BlockSpec((1, Tc, hp * dk), hmap), pl.BlockSpec((1, Tc, hp * dk), hmap),
                  pl.BlockSpec((1, Tc, hp * dv), hmap), gspec(glc),
                  pl.BlockSpec((1, T, hp * dk), hmap), pl.BlockSpec((1, T, hp * dk), hmap),
                  pl.BlockSpec((1, T, hp * dv), hmap), gspec(gll)],
        out_specs=pl.BlockSpec((1, T, hp * dv), hmap),
        out_shape=jax.ShapeDtypeStruct((B, T, H * dv), F32),
        compiler_params=_cparams("parallel", "parallel"), name="mlstm")(qc, kc, vc, glc, ql, kl, vl, gll)


def _gate_layout(gt, nc):
    B = gt.shape[0]
    g = gt[:, :, :4 * ML_HEADS].reshape(B, nc, ML_CHUNK, 4, ML_HEADS)
    return g.transpose(0, 4, 1, 3, 2)


def _odd_out_kernel(hs_ref, og_ref, hg_ref, w_ref, h_ref, gate_ref, o_ref, *, dv):
    hs = hs_ref[0]
    parts = [hs[:, j * dv:(j + 1) * dv] for j in range(hs.shape[1] // dv)]
    normed = jnp.concatenate(
        [p * lax.rsqrt(jnp.mean(p * p, axis=-1, keepdims=True) + EPS) for p in parts], axis=1) * hg_ref[...]
    o_ref[0] = h_ref[0] + gate_ref[0] * _mm((og_ref[0] * normed).astype(BF16), w_ref[...])


def _odd_out(hs, og, hg, w, h, gate, *, dv):
    B, T, D = h.shape
    tm = _tile(T, 512)
    mix = hs.shape[-1]
    g3, gmap = _bvec(gate)
    row = lambda b, i: (b, i, 0)
    const = lambda b, i: (0, 0)
    return pl.pallas_call(
        functools.partial(_odd_out_kernel, dv=dv), grid=(B, T // tm),
        in_specs=[pl.BlockSpec((1, tm, mix), row), pl.BlockSpec((1, tm, mix), row), pl.BlockSpec(hg.shape, const),
                  pl.BlockSpec(w.shape, const), pl.BlockSpec((1, tm, D), row), pl.BlockSpec((1, 1, D), gmap)],
        out_specs=pl.BlockSpec((1, tm, D), row),
        out_shape=jax.ShapeDtypeStruct(h.shape, F32),
        compiler_params=_cparams("parallel", "parallel"), name="odd_out")(hs, og, hg, w, h, g3)


def _rope_tables(seq):
    n_freq = MLA_ROPE // 4
    inv = ROPE_BASE ** (-jnp.arange(n_freq, dtype=F32) / n_freq)
    t = jnp.arange(seq, dtype=jnp.int32)
    row = (t // GRID_W).astype(F32)
    col = (t % GRID_W).astype(F32)
    ang = jnp.concatenate([row[:, None] * inv, col[:, None] * inv], axis=-1)
    ones = jnp.ones((seq, MLA_NOPE), F32)
    tail = jnp.ones((seq, LANE - MLA_NOPE - MLA_ROPE), F32)
    cos = jnp.concatenate([ones, jnp.cos(ang), jnp.cos(ang), tail], axis=-1)
    sin = jnp.concatenate([0 * ones, jnp.sin(ang), jnp.sin(ang), 0 * tail], axis=-1)
    return cos, sin


def _rot_cols(w):
    half = w.shape[-1] // 2
    return jnp.concatenate([-w[..., half:], w[..., :half]], axis=-1)


def _slot(cols, width=LANE):
    pad = width - cols.shape[-1]
    out = jnp.pad(cols, [(0, 0)] * (cols.ndim - 1) + [(0, pad)])
    return out.reshape(out.shape[:-2] + (out.shape[-2] * width,))


def _even_weights(w_in, w_uq, w_ukv, w_out, q_rank, kv_rank):
    D = w_in.shape[0]
    H = MLA_HEADS
    o = q_rank + kv_rank
    w_kr = w_in[:, o:o + MLA_ROPE]
    lead = jnp.zeros((D, MLA_NOPE), F32)
    tail = jnp.zeros((D, LANE - MLA_NOPE - MLA_ROPE), F32)
    win = jnp.concatenate([w_in[:, :o], lead, w_kr, tail, lead, _rot_cols(w_kr), tail, w_in[:, o + MLA_ROPE:]],
                          axis=1).astype(BF16)
    uq = w_uq.reshape(q_rank, H, MLA_NOPE + MLA_ROPE)
    uq_rot = jnp.concatenate([jnp.zeros_like(uq[..., :MLA_NOPE]), _rot_cols(uq[..., MLA_NOPE:])], axis=-1)
    wq2 = jnp.concatenate([_slot(uq), _slot(uq_rot)], axis=1).astype(BF16)
    ukv = w_ukv.reshape(kv_rank, H, MLA_NOPE + MLA_V)
    wkv2 = jnp.concatenate([_slot(ukv[..., :MLA_NOPE]), _slot(ukv[..., MLA_NOPE:])], axis=1).astype(BF16)
    nattn = H * MLA_V
    wa = jnp.pad(w_out[:nattn].reshape(H, MLA_V, D), ((0, 0), (0, LANE - MLA_V), (0, 0))).reshape(H * LANE, D)
    return win, wq2, wkv2, wa.astype(BF16), w_out[nattn:].astype(BF16)


def kernel(x, c, ctx, c_ctx, norm1_g, norm2_g, w_mod, b_mod, even_w_in, mla_q_norm, mla_kv_norm, mla_w_uq, mla_w_ukv, conv_w, even_w_out, odd_w_in, mlstm_gate_b, mlstm_head_g, odd_w_out, peer_w_q, peer_subkeys, peer_u, peer_v, norm_f_g):
    B, S, D = x.shape
    n_ctx = ctx.shape[1]
    depth = norm1_g.shape[0]
    hl, hc = x, ctx
    row2 = lambda v: v[None, :]

    cc = jnp.zeros((-(-(B + 1) // 8) * 8, D), F32).at[:B].set(c).at[B].set(c_ctx)
    cos_l, sin_l = _rope_tables(S)
    cos_c = jnp.ones((n_ctx, LANE), F32)
    sin_c = jnp.zeros((n_ctx, LANE), F32)

    for i in range(depth):
        last = i == depth - 1
        j = i // 2
        mod = _mod_vectors(cc, w_mod[i].astype(BF16), row2(b_mod[i]))
        mod_l = [mod[:B, k * D:(k + 1) * D] for k in range(6)]
        mod_c = [mod[B:B + 1, k * D:(k + 1) * D] for k in range(6)]
        g1, g2 = row2(norm1_g[i]), row2(norm2_g[i])

        if i % 2 == 0:
            q_rank, kv_rank = mla_q_norm.shape[1], mla_kv_norm.shape[1]
            conv_dim = conv_w.shape[-1]
            win, wq2, wkv2, wa, wc = _even_weights(even_w_in[j], mla_w_uq[j], mla_w_ukv[j], even_w_out[j],
                                                   q_rank, kv_rank)
            qn, kvn = row2(mla_q_norm[j]), row2(mla_kv_norm[j])
            kw = dict(conv_dim=conv_dim, q_scale=float((MLA_NOPE + MLA_ROPE) ** -0.5))
            ql, kl, vl, bgl, zl = _even_in(hl, g1, mod_l[0], mod_l[1], win, qn, kvn, wq2, wkv2, cos_l, sin_l, **kw)
            qc, kc, vc, bgc, zc = _even_in(hc, g1, mod_c[0], mod_c[1], win, qn, kvn, wq2, wkv2, cos_c, sin_c, **kw)
            ol = _attention(ql, [kl, kc], [vl, vc])
            hl = _even_out(ol, bgl, zl, hl, mod_l[2], conv_w[j], wa, wc)
            if not last:
                oc = _attention(qc, [kc], [vc])
                hc = _even_out(oc, bgc, zc, hc, mod_c[2], conv_w[j], wa, wc)
        else:
            H = ML_HEADS
            mix = mlstm_head_g.shape[1]
            dv = mix // H
            qkvg = odd_w_in.shape[2] - mix
            dk = (qkvg - 4 * H - mix) // (2 * H)
            qk = H * dk
            w = odd_w_in[j]
            win = jnp.concatenate([w[:, :2 * qk + mix], w[:, qkvg:], w[:, 2 * qk + mix:qkvg],
                                   jnp.zeros((D, LANE - 4 * H), F32)], axis=1).astype(BF16)
            gb = jnp.pad(mlstm_gate_b[j], (0, LANE - 4 * H))[None, :]
            fm = jnp.pad(jnp.tile(jnp.repeat(jnp.array([0.0, 1.0], F32), H), 2), (0, LANE - 4 * H))[None, :]
            kw = dict(qk=qk, vd=mix, q_scale=float(dk ** -0.5))
            ql, kl, vl, ogl, gtl = _odd_in(hl, g1, mod_l[0], mod_l[1], win, gb, fm, **kw)
            qc, kc, vc, ogc, gtc = _odd_in(hc, g1, mod_c[0], mod_c[1], win, gb, fm, **kw)
            gll, glc = _gate_layout(gtl, S // ML_CHUNK), _gate_layout(gtc, n_ctx // ML_CHUNK)
            hs = _mlstm(qc, kc, vc, glc, ql, kl, vl, gll, dk=dk, dv=dv)
            hg, wo = row2(mlstm_head_g[j]), odd_w_out[j].astype(BF16)
            hl = _odd_out(hs, ogl, hg, wo, hl, mod_l[2], dv=dv)
            if not last:
                raise NotImplementedError("context readout of an mLSTM layer is only needed when a layer follows it")

        nheads = peer_subkeys.shape[1]
        wqT = peer_w_q[i].T.astype(BF16)
        sub = peer_subkeys[i].reshape((2 * nheads,) + peer_subkeys.shape[3:]).astype(BF16)
        u, vT = peer_u[i].astype(BF16), peer_v[i].T.astype(BF16)
        nf = row2(norm_f_g)
        hl = _peer(hl, g2, mod_l[3], mod_l[4], mod_l[5], wqT, sub, u, vT, nf, final_norm=last)
        if not last:
            hc = _peer(hc, g2, mod_c[3], mod_c[4], mod_c[5], wqT, sub, u, vT, nf, final_norm=False)
    return hl
```

```python
import functools

import jax
import jax.numpy as jnp
from jax import lax
from jax.experimental import pallas as pl
from jax.experimental.pallas import tpu as pltpu

GRID_W = 64
EPS = 1e-6
ROPE_BASE = 10000.0
MLA_HEADS = 8
MLA_NOPE = 64
MLA_ROPE = 32
MLA_V = 64
ML_HEADS = 4
ML_CHUNK = 64
PEER_TOPK = 16

F32 = jnp.float32
BF16 = jnp.bfloat16
LANE = 128
VMEM_LIMIT = 56 * 1024 * 1024
NEG_INF = float("-inf")


def _cparams(*sem):
    return pltpu.CompilerParams(dimension_semantics=sem, vmem_limit_bytes=VMEM_LIMIT)


def _tile(n, pref):
    t = min(n, pref)
    assert n % t == 0, (n, pref)
    return t


def _rms(x, g):
    return x * lax.rsqrt(jnp.mean(x * x, axis=-1, keepdims=True) + EPS) * g


def _norm_mod(h_ref, g_ref, sh_ref, sc_ref):
    return _rms(h_ref[0], g_ref[...]) * (1.0 + sc_ref[0]) + sh_ref[0]


def _mm(a, b):
    return jnp.dot(a, b, preferred_element_type=F32)


def _mm_nt(a, b):
    return lax.dot_general(a, b, (((1,), (1,)), ((), ())), preferred_element_type=F32)


def _mod_kernel(c_ref, w_ref, b_ref, o_ref):
    c = c_ref[...]
    o_ref[...] = _mm((c * jax.nn.sigmoid(c)).astype(BF16), w_ref[...]) + b_ref[...]


def _mod_vectors(cc, w, b):
    R, D = cc.shape
    N = w.shape[1]
    tn = _tile(N, 1024)
    return pl.pallas_call(
        _mod_kernel, grid=(N // tn,),
        in_specs=[pl.BlockSpec((R, D), lambda j: (0, 0)),
                  pl.BlockSpec((D, tn), lambda j: (0, j)),
                  pl.BlockSpec((1, tn), lambda j: (0, j))],
        out_specs=pl.BlockSpec((R, tn), lambda j: (0, j)),
        out_shape=jax.ShapeDtypeStruct((R, N), F32),
        compiler_params=_cparams("parallel"), name="mod_vectors")(cc, w, b)


def _bvec(v):
    v3 = v[:, None, :]
    if v3.shape[0] == 1:
        return v3, (lambda b, *_: (0, 0, 0))
    return v3, (lambda b, *_: (b, 0, 0))


def _even_in_kernel(h_ref, g_ref, sh_ref, sc_ref, win_ref, qn_ref, kvn_ref, wq2_ref, wkv2_ref, cos_ref, sin_ref,
                    q_ref, k_ref, v_ref, bg_ref, z_ref, *, q_rank, kv_rank, conv_dim, q_scale):
    nl = _norm_mod(h_ref, g_ref, sh_ref, sc_ref)
    p = _mm(nl.astype(BF16), win_ref[...])
    o = q_rank + kv_rank
    pq, pc = p[:, :q_rank], p[:, q_rank:o]
    kr, krr = p[:, o:o + LANE], p[:, o + LANE:o + 2 * LANE]
    o += 2 * LANE
    bg, cg, u = p[:, o:o + conv_dim], p[:, o + conv_dim:o + 2 * conv_dim], p[:, o + 2 * conv_dim:o + 3 * conv_dim]
    cos, sin = cos_ref[...], sin_ref[...]
    hw = MLA_HEADS * LANE
    cos_t, sin_t = jnp.tile(cos, (1, MLA_HEADS)), jnp.tile(sin, (1, MLA_HEADS))
    qq = _mm(_rms(pq, qn_ref[...]).astype(BF16), wq2_ref[...])
    q_ref[0] = ((qq[:, :hw] * cos_t + qq[:, hw:] * sin_t) * q_scale).astype(BF16)
    kv = _mm(_rms(pc, kvn_ref[...]).astype(BF16), wkv2_ref[...])
    k_ref[0] = (kv[:, :hw] + jnp.tile(kr * cos + krr * sin, (1, MLA_HEADS))).astype(BF16)
    v_ref[0] = kv[:, hw:].astype(BF16)
    bg_ref[0] = bg
    z_ref[0] = cg * u


def _even_in(h, g, shift, scale, win, qn, kvn, wq2, wkv2, cos, sin, *, conv_dim, q_scale):
    B, T, D = h.shape
    tm = _tile(T, 512)
    hw = MLA_HEADS * LANE
    sh3, vmap_ = _bvec(shift)
    sc3, _ = _bvec(scale)
    const = lambda b, i: (0, 0)
    row = lambda b, i: (b, i, 0)
    kern = functools.partial(_even_in_kernel, q_rank=qn.shape[1], kv_rank=kvn.shape[1], conv_dim=conv_dim,
                             q_scale=q_scale)
    return pl.pallas_call(
        kern, grid=(B, T // tm),
        in_specs=[pl.BlockSpec((1, tm, D), row), pl.BlockSpec(g.shape, const),
                  pl.BlockSpec((1, 1, D), vmap_), pl.BlockSpec((1, 1, D), vmap_),
                  pl.BlockSpec(win.shape, const), pl.BlockSpec(qn.shape, const), pl.BlockSpec(kvn.shape, const),
                  pl.BlockSpec(wq2.shape, const), pl.BlockSpec(wkv2.shape, const),
                  pl.BlockSpec((tm, LANE), lambda b, i: (i, 0)), pl.BlockSpec((tm, LANE), lambda b, i: (i, 0))],
        out_specs=[pl.BlockSpec((1, tm, hw), row)] * 3 + [pl.BlockSpec((1, tm, conv_dim), row)] * 2,
        out_shape=[jax.ShapeDtypeStruct((B, T, hw), BF16)] * 3 + [jax.ShapeDtypeStruct((B, T, conv_dim), F32)] * 2,
        compiler_params=_cparams("parallel", "parallel"), name="even_in")(
            h, g, sh3, sc3, win, qn, kvn, wq2, wkv2, cos, sin)


def _attn_kernel(q_ref, *refs, nsets, hp):
    k_refs, v_refs, o_ref = refs[:nsets], refs[nsets:2 * nsets], refs[2 * nsets]
    heads = [slice(p * LANE, (p + 1) * LANE) for p in range(hp)]
    scores = [[_mm_nt(q_ref[0, :, lanes], k[0, :, lanes]) for k in k_refs] for lanes in heads]
    for lanes, ss in zip(heads, scores):
        m = functools.reduce(jnp.maximum, [jnp.max(s, axis=-1, keepdims=True) for s in ss])
        ps = [jnp.exp(s - m) for s in ss]
        l = functools.reduce(jnp.add, [jnp.sum(pr, axis=-1, keepdims=True) for pr in ps])
        o = functools.reduce(jnp.add, [_mm(pr.astype(BF16), v[0, :, lanes]) for pr, v in zip(ps, v_refs)])
        o_ref[0, :, lanes] = (o / l).astype(BF16)


def _attention(q, ks, vs):
    B, T, _ = q.shape
    tq = _tile(T, 256)
    hp = 2
    qmap = lambda b, h, i: (b, i, h)
    kmap = lambda b, h, i: (b, 0, h)
    kspecs = [pl.BlockSpec((1, k.shape[1], hp * LANE), kmap) for k in ks]
    return pl.pallas_call(
        functools.partial(_attn_kernel, nsets=len(ks), hp=hp), grid=(B, MLA_HEADS // hp, T // tq),
        in_specs=[pl.BlockSpec((1, tq, hp * LANE), qmap)] + kspecs + kspecs,
        out_specs=pl.BlockSpec((1, tq, hp * LANE), qmap),
        out_shape=jax.ShapeDtypeStruct(q.shape, BF16),
        compiler_params=_cparams("parallel", "parallel", "parallel"), name="mla_attention")(q, *ks, *vs)


def _even_out_kernel(a_ref, bg_ref, z_ref, zp_ref, zn_ref, h_ref, gate_ref, cw_ref, wa_ref, wc_ref, o_ref):
    i, last = pl.program_id(1), pl.num_programs(1) - 1
    z = z_ref[0]
    tm = z.shape[0]
    rows = lax.broadcasted_iota(jnp.int32, z.shape, 0)
    prev_row = zp_ref[0][7:8, :] * (i > 0).astype(F32)
    next_row = zn_ref[0][0:1, :] * (i < last).astype(F32)
    zm1 = jnp.where(rows == 0, prev_row, pltpu.roll(z, 1, 0))
    zp1 = jnp.where(rows == tm - 1, next_row, pltpu.roll(z, tm - 1, 0))
    cw = cw_ref[...]
    y = cw[0:1, :] * zm1 + cw[1:2, :] * z + cw[2:3, :] * zp1
    o = _mm(a_ref[0], wa_ref[...]) + _mm((bg_ref[0] * y).astype(BF16), wc_ref[...])
    o_ref[0] = h_ref[0] + gate_ref[0] * o


def _even_out(attn, bg, z, h, gate, cw, wa, wc):
    B, T, D = h.shape
    tm = _tile(T, 512)
    cd = z.shape[-1]
    nb8 = T // 8
    g3, gmap = _bvec(gate)
    row = lambda b, i: (b, i, 0)
    const = lambda b, i: (0, 0)
    return pl.pallas_call(
        _even_out_kernel, grid=(B, T // tm),
        in_specs=[pl.BlockSpec((1, tm, attn.shape[-1]), row), pl.BlockSpec((1, tm, cd), row),
                  pl.BlockSpec((1, tm, cd), row),
                  pl.BlockSpec((1, 8, cd), lambda b, i: (b, jnp.maximum(i * (tm // 8) - 1, 0), 0)),
                  pl.BlockSpec((1, 8, cd), lambda b, i: (b, jnp.minimum((i + 1) * (tm // 8), nb8 - 1), 0)),
                  pl.BlockSpec((1, tm, D), row), pl.BlockSpec((1, 1, D), gmap),
                  pl.BlockSpec(cw.shape, const), pl.BlockSpec(wa.shape, const), pl.BlockSpec(wc.shape, const)],
        out_specs=pl.BlockSpec((1, tm, D), row),
        out_shape=jax.ShapeDtypeStruct(h.shape, F32),
        compiler_params=_cparams("parallel", "parallel"), name="even_out")(
            attn, bg, z, z, z, h, g3, cw, wa, wc)


def _batcher_network(n):
    def merge(lo, hi, r):
        step = r * 2
        if step < hi - lo:
            yield from merge(lo, hi, step)
            yield from merge(lo + r, hi, step)
            yield from ((i, i + r) for i in range(lo + r, hi - r, step))
        else:
            yield (lo, lo + r)

    def sort(lo, hi):
        if hi - lo >= 1:
            mid = lo + (hi - lo) // 2
            yield from sort(lo, mid)
            yield from sort(mid + 1, hi)
            yield from merge(lo, hi, 1)

    assert n & (n - 1) == 0, n
    return list(sort(0, n - 1))


def _peer_topk_head(s1, s2, ab_scr):
    K = PEER_TOPK
    sub = 8
    assert s1.shape[0] == K * sub, s1.shape
    slabs = [s1[v * sub:(v + 1) * sub, :] for v in range(K)]
    for i, j in _batcher_network(K):
        slabs[i], slabs[j] = jnp.maximum(slabs[i], slabs[j]), jnp.minimum(slabs[i], slabs[j])
    for r in range(K):
        m = jnp.max(slabs[0], axis=0, keepdims=True)
        ab_scr[r:r + 1, :] = m
        hit = slabs[0] == m
        for v in range(K - 1 - r):
            slabs[v] = jnp.where(hit, slabs[v + 1], slabs[v])
    rank2 = jnp.full(s2.shape, float(K), F32)
    work = s2
    for r in range(K):
        m = jnp.max(work, axis=0, keepdims=True)
        ab_scr[K + r:K + r + 1, :] = m
        hit = work == m
        rank2 = jnp.where(hit, float(r), rank2)
        if r + 1 < K:
            work = jnp.where(hit, NEG_INF, work)
    bgrp = [ab_scr[K:K + 8, :], ab_scr[K + 8:K + 16, :]]
    a_rows = [ab_scr[i:i + 1, :] for i in range(K)]
    groups = []
    for i in range(K):
        for bj in (bgrp if i < 2 else bgrp[:1]):
            groups.append(a_rows[i] + bj)
    top = a_rows[0] + ab_scr[K:K + 1, :]
    zsum = jnp.zeros_like(top)
    tau = top
    work = groups
    for r in range(K):
        m = jnp.max(functools.reduce(jnp.maximum, work), axis=0, keepdims=True)
        zsum = zsum + jnp.exp(m - top)
        tau = m
        if r + 1 < K:
            work = [jnp.where(g == m, NEG_INF, g) for g in work]
    cnt1 = jnp.zeros(s1.shape, F32)
    for i in range(K):
        t = None
        for bj in (bgrp if i < 2 else bgrp[:1]):
            c = jnp.where(a_rows[i] + bj >= tau, 1.0, 0.0)
            t = c if t is None else t + c
        cnt1 = jnp.where(s1 == a_rows[i], jnp.sum(t, axis=0, keepdims=True), cnt1)
    c1 = jnp.where(s1 >= a_rows[K - 1], jnp.exp(s1 - a_rows[0]) / zsum, 0.0)
    e2 = jnp.exp(s2 - ab_scr[K:K + 1, :])
    return rank2, cnt1, c1, e2


def _peer_route_kernel(h_ref, g_ref, sh_ref, sc_ref, wqT_ref, sub_ref, xT_ref, r2_ref, e2_ref, n1_ref, c1_ref,
                       q_scr, s_scr, ab_scr, *, nheads):
    xT = _norm_mod(h_ref, g_ref, sh_ref, sc_ref).T.astype(BF16)
    xT_ref[0, 0] = xT
    q_scr[...] = _mm(wqT_ref[...], xT)
    dh = sub_ref.shape[2]
    for hp in range(2 * nheads):
        s_scr[hp] = _mm(sub_ref[hp], q_scr[hp * dh:(hp + 1) * dh, :].astype(BF16))

    def head_body(hh, carry):
        rank2, cnt1, c1, e2 = _peer_topk_head(s_scr[2 * hh], s_scr[2 * hh + 1], ab_scr)
        r2_ref[0, 0, hh] = rank2.astype(BF16)
        e2_ref[0, 0, hh] = e2.astype(BF16)
        n1_ref[0, 0, hh] = cnt1
        c1_ref[0, 0, hh] = c1
        return carry

    lax.fori_loop(0, nheads, head_body, 0)


def _peer_gate_tiles(key0, nsub, g_scr, r2_ref, e2_ref, n1_ref, c1_ref, *, nheads, nkeys, sub_rows=16):
    tb = g_scr.shape[1]
    lane_w = min(tb, 2 * LANE)
    zero = jnp.zeros((sub_rows, lane_w), BF16)
    for l0 in range(0, tb, lane_w):
        lanes = slice(l0, l0 + lane_w)
        row_bf16 = lambda ref, hh, j: jnp.broadcast_to(ref[0, 0, hh, key0 + j:key0 + j + 1, lanes],
                                                       (sub_rows, lane_w)).astype(BF16)
        cnt = [[row_bf16(n1_ref, hh, j) for j in range(nsub)] for hh in range(nheads)]
        cc = [[row_bf16(c1_ref, hh, j) for j in range(nsub)] for hh in range(nheads)]
        for rg in range(nkeys // sub_rows):
            rows = slice(rg * sub_rows, (rg + 1) * sub_rows)
            accs = [None] * nsub
            for hh in range(nheads):
                r2t, e2t = r2_ref[0, 0, hh, rows, lanes], e2_ref[0, 0, hh, rows, lanes]
                for j in range(nsub):
                    t = jnp.where(r2t < cnt[hh][j], e2t, zero) * cc[hh][j]
                    accs[j] = t if accs[j] is None else accs[j] + t
            for j in range(nsub):
                lo = (key0 + j) * nkeys + rg * sub_rows
                g_scr[lo:lo + sub_rows, lanes] = accs[j]


def _peer_expert_kernel(h_ref, gate_ref, nf_ref, xT_ref, r2_ref, e2_ref, n1_ref, c1_ref, u_ref, vTp_ref, vTl_ref,
                        o_ref, acc_scr, g_scr, w_scr, *, nheads, nkeys, final_norm):
    c, nch = pl.program_id(2), pl.num_programs(2)

    @pl.when(c == 0)
    def _init():
        acc_scr[...] = jnp.zeros_like(acc_scr)
        w_scr[...] = jnp.zeros_like(w_scr)

    nsub = u_ref.shape[0] // nkeys
    group = min(nsub, 4)
    for k0 in range(0, nsub, group):
        _peer_gate_tiles(k0, group, g_scr, r2_ref, e2_ref, n1_ref, c1_ref, nheads=nheads, nkeys=nkeys)
    acc_scr[...] += _mm(vTp_ref[...], w_scr[...])
    act = _mm(u_ref[...], xT_ref[0, 0])
    ge = 0.5 * act * (1.0 + lax.erf(act * (2.0 ** -0.5)))
    w_scr[...] = g_scr[...] * ge.astype(BF16)

    @pl.when(c == nch - 1)
    def _epilogue():
        acc = acc_scr[...] + _mm(vTl_ref[...], w_scr[...])
        hn = h_ref[0] + gate_ref[0] * acc.T
        if final_norm:
            hn = _rms(hn, nf_ref[...])
        o_ref[0] = hn


def _peer(h, g, shift, scale, gate, wqT, sub, u, vT, nf, *, final_norm):
    B, T, D = h.shape
    Tb = _tile(T, 512)
    nexp = u.shape[0]
    nkeys = sub.shape[1]
    nheads = sub.shape[0] // 2
    ec = _tile(nexp, 1024)
    nst, nblk, nsub = nexp // ec, T // Tb, ec // nkeys
    sh3, vmap2 = _bvec(shift)
    sc3, _ = _bvec(scale)
    g3, vmap3 = _bvec(gate)

    blk2 = lambda b, i: (b, i, 0, 0, 0)
    hshape = (B, nblk, nheads, nkeys, Tb)
    hspec2 = pl.BlockSpec((1, 1, nheads, nkeys, Tb), blk2)
    xT, r2, e2, n1, c1 = pl.pallas_call(
        functools.partial(_peer_route_kernel, nheads=nheads), grid=(B, nblk),
        in_specs=[pl.BlockSpec((1, Tb, D), lambda b, i: (b, i, 0)), pl.BlockSpec(g.shape, lambda b, i: (0, 0)),
                  pl.BlockSpec((1, 1, D), vmap2), pl.BlockSpec((1, 1, D), vmap2),
                  pl.BlockSpec(wqT.shape, lambda b, i: (0, 0)), pl.BlockSpec(sub.shape, lambda b, i: (0, 0, 0))],
        out_specs=[pl.BlockSpec((1, 1, D, Tb), lambda b, i: (b, i, 0, 0))] + [hspec2] * 4,
        out_shape=[jax.ShapeDtypeStruct((B, nblk, D, Tb), BF16)]
        + [jax.ShapeDtypeStruct(hshape, dt) for dt in (BF16, BF16, F32, F32)],
        scratch_shapes=[pltpu.VMEM((wqT.shape[0], Tb), F32), pltpu.VMEM((2 * nheads, nkeys, Tb), F32),
                        pltpu.VMEM((2 * PEER_TOPK, Tb), F32)],
        compiler_params=_cparams("parallel", "parallel"), name="peer_route")(h, g, sh3, sc3, wqT, sub)

    row = lambda b, i, c: (b, i, 0)
    const2 = lambda b, i, c: (0, 0)
    blk3 = lambda b, i, c: (b, i, 0, 0, 0)
    hspec3 = pl.BlockSpec((1, 1, nheads, nkeys, Tb), blk3)
    cspec3 = pl.BlockSpec((1, 1, nheads, nsub, Tb), lambda b, i, c: (b, i, 0, c, 0))
    kern = functools.partial(_peer_expert_kernel, nheads=nheads, nkeys=nkeys, final_norm=final_norm)
    return pl.pallas_call(
        kern, grid=(B, nblk, nst),
        in_specs=[pl.BlockSpec((1, Tb, D), row), pl.BlockSpec((1, 1, D), vmap3), pl.BlockSpec(nf.shape, const2),
                  pl.BlockSpec((1, 1, D, Tb), lambda b, i, c: (b, i, 0, 0)), hspec3, hspec3, cspec3, cspec3,
                  pl.BlockSpec((ec, D), lambda b, i, c: (c, 0)),
                  pl.BlockSpec((D, ec), lambda b, i, c: (0, jnp.maximum(c - 1, 0))),
                  pl.BlockSpec((D, ec), lambda b, i, c: (0, jnp.where(c == nst - 1, nst - 1, 0)))],
        out_specs=pl.BlockSpec((1, Tb, D), row),
        out_shape=jax.ShapeDtypeStruct(h.shape, F32),
        scratch_shapes=[pltpu.VMEM((D, Tb), F32), pltpu.VMEM((ec, Tb), BF16), pltpu.VMEM((ec, Tb), BF16)],
        compiler_params=_cparams("parallel", "parallel", "arbitrary"), name="peer_experts")(
            h, g3, nf, xT, r2, e2, n1, c1, u, vT, vT)


def _odd_in_kernel(h_ref, g_ref, sh_ref, sc_ref, win_ref, gb_ref, fm_ref, q_ref, k_ref, v_ref, og_ref, gt_ref,
                   *, qk, vd, q_scale):
    nl = _norm_mod(h_ref, g_ref, sh_ref, sc_ref)
    p = _mm(nl.astype(BF16), win_ref[...])
    q_ref[0] = (p[:, :qk] * q_scale).astype(BF16)
    k_ref[0] = p[:, qk:2 * qk].astype(BF16)
    v_ref[0] = p[:, 2 * qk:2 * qk + vd].astype(BF16)
    og_ref[0] = jax.nn.sigmoid(p[:, 2 * qk + vd:2 * qk + 2 * vd])
    gt = p[:, 2 * qk + 2 * vd:] + gb_ref[...]
    log_sig = jnp.minimum(gt, 0.0) - jnp.log1p(jnp.exp(-jnp.abs(gt)))
    gt_ref[0] = jnp.where(fm_ref[...] > 0.5, log_sig, gt)


def _odd_in(h, g, shift, scale, win, gb, fm, *, qk, vd, q_scale):
    B, T, D = h.shape
    tm = _tile(T, 512)
    sh3, vmap_ = _bvec(shift)
    sc3, _ = _bvec(scale)
    const = lambda b, i: (0, 0)
    row = lambda b, i: (b, i, 0)
    widths = (qk, qk, vd, vd, LANE)
    dts = (BF16, BF16, BF16, F32, F32)
    return pl.pallas_call(
        functools.partial(_odd_in_kernel, qk=qk, vd=vd, q_scale=q_scale), grid=(B, T // tm),
        in_specs=[pl.BlockSpec((1, tm, D), row), pl.BlockSpec(g.shape, const),
                  pl.BlockSpec((1, 1, D), vmap_), pl.BlockSpec((1, 1, D), vmap_),
                  pl.BlockSpec(win.shape, const), pl.BlockSpec(gb.shape, const), pl.BlockSpec(fm.shape, const)],
        out_specs=[pl.BlockSpec((1, tm, w), row) for w in widths],
        out_shape=[jax.ShapeDtypeStruct((B, T, w), dt) for w, dt in zip(widths, dts)],
        compiler_params=_cparams("parallel", "parallel"), name="odd_in")(h, g, sh3, sc3, win, gb, fm)


def _mlstm_chunks(qs, ks, vs, gls, states, dirs, with_out):
    G = range(len(dirs))
    L = qs[0].shape[0]
    r = lax.broadcasted_iota(jnp.int32, (L, L), 0)
    cidx = lax.broadcasted_iota(jnp.int32, (L, L), 1)
    eye = (r == cidx).astype(F32)
    seen = {0: cidx <= r, 1: cidx >= r}
    seen_f = {d: seen[d].astype(F32) for d in set(dirs)}
    seen_t = {d: seen[1 - d].astype(F32) for d in set(dirs)}
    hi = lax.Precision.HIGHEST
    nt = (((1,), (1,)), ((), ()))
    Cs, ns, ms = zip(*states)
    ir = [gls[g][2 * dirs[g]:2 * dirs[g] + 1, :] for g in G]
    frm = [jnp.broadcast_to(gls[g][2 * dirs[g] + 1:2 * dirs[g] + 2, :], (L, L)) for g in G]
    irm = [jnp.broadcast_to(ir[g], (L, L)) for g in G]
    bcm = [lax.dot_general(seen_f[dirs[g]], frm[g], nt, precision=hi, preferred_element_type=F32) for g in G]
    brm = [jnp.dot(frm[g], seen_t[dirs[g]], precision=hi, preferred_element_type=F32) for g in G]
    icm = [lax.dot_general(eye, irm[g], nt, precision=hi, preferred_element_type=F32) for g in G]
    bcol = [bcm[g][:, 0:1] for g in G]
    brow = [brm[g][0:1, :] for g in G]
    b_end = [bcol[g][L - 1:L, :] if dirs[g] == 0 else bcol[g][0:1, :] for g in G]
    a_row = [ir[g] + b_end[g] - brow[g] for g in G]
    a_col = [icm[g][:, 0:1] + b_end[g] - bcol[g] for g in G]
    m_new = [jnp.maximum(b_end[g] + ms[g], jnp.max(a_row[g], axis=1, keepdims=True)) for g in G]
    decay = [jnp.exp(b_end[g] + ms[g] - m_new[g]) for g in G]
    kw = [ks[g].astype(F32) * jnp.exp(a_col[g] - m_new[g]) for g in G]
    upd = [lax.dot_general(kw[g].astype(BF16), vs[g], (((0,), (0,)), ((), ())), preferred_element_type=F32)
           for g in G]
    new_states = [(decay[g] * Cs[g] + upd[g], decay[g] * ns[g] + jnp.sum(kw[g], axis=0, keepdims=True), m_new[g])
                  for g in G]
    if not with_out:
        return new_states, None
    dm = [jnp.where(seen[dirs[g]], bcm[g] - brm[g] + ir[g], NEG_INF) for g in G]
    m_t = [jnp.maximum(bcol[g] + ms[g], jnp.max(dm[g], axis=1, keepdims=True)) for g in G]
    inter = [jnp.exp(bcol[g] + ms[g] - m_t[g]) for g in G]
    sc = [_mm_nt(qs[g], ks[g]) * jnp.exp(dm[g] - m_t[g]) for g in G]
    intra = [_mm(sc[g].astype(BF16), vs[g]) for g in G]
    cross = [_mm(qs[g], Cs[g].astype(BF16)) for g in G]
    den = [jnp.sum(sc[g], axis=1, keepdims=True)
           + inter[g] * jnp.sum(qs[g].astype(F32) * ns[g], axis=1, keepdims=True) for g in G]
    outs = [(intra[g] + inter[g] * cross[g]) / jnp.maximum(jnp.abs(den[g]), jnp.exp(-m_t[g])) for g in G]
    return new_states, outs


def _mlstm_kernel(qc_ref, kc_ref, vc_ref, glc_ref, ql_ref, kl_ref, vl_ref, gll_ref, o_ref, *, hp, dk, dv):
    L = ML_CHUNK
    ncc, ncl = glc_ref.shape[2], gll_ref.shape[2]
    chains = [(p, d) for p in range(hp) for d in (0, 1)]

    def scan(q_ref, k_ref, v_ref, gl_ref, nc, lo, hi, states, out_mode):
        def body(j, sts):
            cjs = [j if d == 0 else nc - 1 - j for _, d in chains]
            sls = [pl.ds(pl.multiple_of(cj * L, L), L) for cj in cjs]
            qs = [q_ref[0, sl, p * dk:(p + 1) * dk] for (p, _), sl in zip(chains, sls)]
            ks = [k_ref[0, sl, p * dk:(p + 1) * dk] for (p, _), sl in zip(chains, sls)]
            vs = [v_ref[0, sl, p * dv:(p + 1) * dv] for (p, _), sl in zip(chains, sls)]
            gls = [gl_ref[0, p, cj] for (p, _), cj in zip(chains, cjs)]
            prev = [o_ref[0, sl, p * dv:(p + 1) * dv] for (p, _), sl in zip(chains, sls)] if out_mode == "add" else None
            new, outs = _mlstm_chunks(qs, ks, vs, gls, sts, [d for _, d in chains], out_mode is not None)
            if outs is not None:
                for g, ((p, _), sl) in enumerate(zip(chains, sls)):
                    tot = outs[g] if prev is None else prev[g].astype(F32) + outs[g]
                    o_ref[0, sl, p * dv:(p + 1) * dv] = tot.astype(o_ref.dtype)
            return tuple(new)

        return lax.fori_loop(lo, hi, body, states)

    assert ncl % 2 == 0, ncl
    st = tuple((jnp.zeros((dk, dv), F32), jnp.zeros((1, dk), F32), jnp.zeros((1, 1), F32)) for _ in chains)
    st = scan(qc_ref, kc_ref, vc_ref, glc_ref, ncc, 0, ncc, st, None)
    st = scan(ql_ref, kl_ref, vl_ref, gll_ref, ncl, 0, ncl // 2, st, "assign")
    scan(ql_ref, kl_ref, vl_ref, gll_ref, ncl, ncl // 2, ncl, st, "add")


def _mlstm(qc, kc, vc, glc, ql, kl, vl, gll, *, dk, dv):
    B, T, _ = ql.shape
    Tc = qc.shape[1]
    H = ML_HEADS
    hp = 4
    hmap = lambda b, h: (b, 0, h)
    gspec = lambda a: pl.BlockSpec((1, hp) + a.shape[2:], lambda b, h: (b, h, 0, 0, 0))
    return pl.pallas_call(
        functools.partial(_mlstm_kernel, hp=hp, dk=dk, dv=dv), grid=(B, H // hp),
        in_specs=[pl.BlockSpec((1, Tc, hp * dk), hmap), pl.BlockSpec((1, Tc, hp * dk), hmap),
                  pl.BlockSpec((1, Tc, hp * dv), hmap), gspec(glc),
                  pl.BlockSpec((1, T, hp * dk), hmap), pl.BlockSpec((1, T, hp * dk), hmap),
                  pl.BlockSpec((1, T, hp * dv), hmap), gspec(gll)],
        out_specs=pl.BlockSpec((1, T, hp * dv), hmap),
        out_shape=jax.ShapeDtypeStruct((B, T, H * dv), BF16),
        compiler_params=_cparams("parallel", "parallel"), name="mlstm")(qc, kc, vc, glc, ql, kl, vl, gll)


def _gate_layout(gt, nc):
    B = gt.shape[0]
    g = gt[:, :, :4 * ML_HEADS].reshape(B, nc, ML_CHUNK, 4, ML_HEADS)
    return g.transpose(0, 4, 1, 3, 2)


def _odd_out_kernel(hs_ref, og_ref, hg_ref, w_ref, h_ref, gate_ref, o_ref, *, dv):
    hs = hs_ref[0].astype(F32)
    parts = [hs[:, j * dv:(j + 1) * dv] for j in range(hs.shape[1] // dv)]
    normed = jnp.concatenate(
        [p * lax.rsqrt(jnp.mean(p * p, axis=-1, keepdims=True) + EPS) for p in parts], axis=1) * hg_ref[...]
    o_ref[0] = h_ref[0] + gate_ref[0] * _mm((og_ref[0] * normed).astype(BF16), w_ref[...])


def _odd_out(hs, og, hg, w, h, gate, *, dv):
    B, T, D = h.shape
    tm = _tile(T, 512)
    mix = hs.shape[-1]
    g3, gmap = _bvec(gate)
    row = lambda b, i: (b, i, 0)
    const = lambda b, i: (0, 0)
    return pl.pallas_call(
        functools.partial(_odd_out_kernel, dv=dv), grid=(B, T // tm),
        in_specs=[pl.BlockSpec((1, tm, mix), row), pl.BlockSpec((1, tm, mix), row), pl.BlockSpec(hg.shape, const),
                  pl.BlockSpec(w.shape, const), pl.BlockSpec((1, tm, D), row), pl.BlockSpec((1, 1, D), gmap)],
        out_specs=pl.BlockSpec((1, tm, D), row),
        out_shape=jax.ShapeDtypeStruct(h.shape, F32),
        compiler_params=_cparams("parallel", "parallel"), name="odd_out")(hs, og, hg, w, h, g3)


def _rope_tables(seq):
    n_freq = MLA_ROPE // 4
    inv = ROPE_BASE ** (-jnp.arange(n_freq, dtype=F32) / n_freq)
    t = jnp.arange(seq, dtype=jnp.int32)
    row = (t // GRID_W).astype(F32)
    col = (t % GRID_W).astype(F32)
    ang = jnp.concatenate([row[:, None] * inv, col[:, None] * inv], axis=-1)
    ones = jnp.ones((seq, MLA_NOPE), F32)
    tail = jnp.ones((seq, LANE - MLA_NOPE - MLA_ROPE), F32)
    cos = jnp.concatenate([ones, jnp.cos(ang), jnp.cos(ang), tail], axis=-1)
    sin = jnp.concatenate([0 * ones, jnp.sin(ang), jnp.sin(ang), 0 * tail], axis=-1)
    return cos, sin


def _rot_cols(w):
    half = w.shape[-1] // 2
    return jnp.concatenate([-w[..., half:], w[..., :half]], axis=-1)


def _slot(cols, width=LANE):
    pad = width - cols.shape[-1]
    out = jnp.pad(cols, [(0, 0)] * (cols.ndim - 1) + [(0, pad)])
    return out.reshape(out.shape[:-2] + (out.shape[-2] * width,))


def _even_weights(w_in, w_uq, w_ukv, w_out, q_rank, kv_rank):
    D = w_in.shape[0]
    H = MLA_HEADS
    o = q_rank + kv_rank
    w_kr = w_in[:, o:o + MLA_ROPE]
    lead = jnp.zeros((D, MLA_NOPE), F32)
    tail = jnp.zeros((D, LANE - MLA_NOPE - MLA_ROPE), F32)
    win = jnp.concatenate([w_in[:, :o], lead, w_kr, tail, lead, _rot_cols(w_kr), tail, w_in[:, o + MLA_ROPE:]],
                          axis=1).astype(BF16)
    uq = w_uq.reshape(q_rank, H, MLA_NOPE + MLA_ROPE)
    uq_rot = jnp.concatenate([jnp.zeros_like(uq[..., :MLA_NOPE]), _rot_cols(uq[..., MLA_NOPE:])], axis=-1)
    wq2 = jnp.concatenate([_slot(uq), _slot(uq_rot)], axis=1).astype(BF16)
    ukv = w_ukv.reshape(kv_rank, H, MLA_NOPE + MLA_V)
    wkv2 = jnp.concatenate([_slot(ukv[..., :MLA_NOPE]), _slot(ukv[..., MLA_NOPE:])], axis=1).astype(BF16)
    nattn = H * MLA_V
    wa = jnp.pad(w_out[:nattn].reshape(H, MLA_V, D), ((0, 0), (0, LANE - MLA_V), (0, 0))).reshape(H * LANE, D)
    return win, wq2, wkv2, wa.astype(BF16), w_out[nattn:].astype(BF16)


def kernel(x, c, ctx, c_ctx, norm1_g, norm2_g, w_mod, b_mod, even_w_in, mla_q_norm, mla_kv_norm, mla_w_uq, mla_w_ukv, conv_w, even_w_out, odd_w_in, mlstm_gate_b, mlstm_head_g, odd_w_out, peer_w_q, peer_subkeys, peer_u, peer_v, norm_f_g):
    B, S, D = x.shape
    n_ctx = ctx.shape[1]
    depth = norm1_g.shape[0]
    hl, hc = x, ctx
    row2 = lambda v: v[None, :]

    cc = jnp.zeros((-(-(B + 1) // 8) * 8, D), F32).at[:B].set(c).at[B].set(c_ctx)
    cos_l, sin_l = _rope_tables(S)
    cos_c = jnp.ones((n_ctx, LANE), F32)
    sin_c = jnp.zeros((n_ctx, LANE), F32)

    for i in range(depth):
        last = i == depth - 1
        j = i // 2
        mod = _mod_vectors(cc, w_mod[i].astype(BF16), row2(b_mod[i]))
        mod_l = [mod[:B, k * D:(k + 1) * D] for k in range(6)]
        mod_c = [mod[B:B + 1, k * D:(k + 1) * D] for k in range(6)]
        g1, g2 = row2(norm1_g[i]), row2(norm2_g[i])

        if i % 2 == 0:
            q_rank, kv_rank = mla_q_norm.shape[1], mla_kv_norm.shape[1]
            conv_dim = conv_w.shape[-1]
            win, wq2, wkv2, wa, wc = _even_weights(even_w_in[j], mla_w_uq[j], mla_w_ukv[j], even_w_out[j],
                                                   q_rank, kv_rank)
            qn, kvn = row2(mla_q_norm[j]), row2(mla_kv_norm[j])
            kw = dict(conv_dim=conv_dim, q_scale=float((MLA_NOPE + MLA_ROPE) ** -0.5))
            ql, kl, vl, bgl, zl = _even_in(hl, g1, mod_l[0], mod_l[1], win, qn, kvn, wq2, wkv2, cos_l, sin_l, **kw)
            qc, kc, vc, bgc, zc = _even_in(hc, g1, mod_c[0], mod_c[1], win, qn, kvn, wq2, wkv2, cos_c, sin_c, **kw)
            ol = _attention(ql, [kl, kc], [vl, vc])
            hl = _even_out(ol, bgl, zl, hl, mod_l[2], conv_w[j], wa, wc)
            if not last:
                oc = _attention(qc, [kc], [vc])
                hc = _even_out(oc, bgc, zc, hc, mod_c[2], conv_w[j], wa, wc)
        else:
            H = ML_HEADS
            mix = mlstm_head_g.shape[1]
            dv = mix // H
            qkvg = odd_w_in.shape[2] - mix
            dk = (qkvg - 4 * H - mix) // (2 * H)
            qk = H * dk
            w = odd_w_in[j]
            win = jnp.concatenate([w[:, :2 * qk + mix], w[:, qkvg:], w[:, 2 * qk + mix:qkvg],
                                   jnp.zeros((D, LANE - 4 * H), F32)], axis=1).astype(BF16)
            gb = jnp.pad(mlstm_gate_b[j], (0, LANE - 4 * H))[None, :]
            fm = jnp.pad(jnp.tile(jnp.repeat(jnp.array([0.0, 1.0], F32), H), 2), (0, LANE - 4 * H))[None, :]
            kw = dict(qk=qk, vd=mix, q_scale=float(dk ** -0.5))
            ql, kl, vl, ogl, gtl = _odd_in(hl, g1, mod_l[0], mod_l[1], win, gb, fm, **kw)
            qc, kc, vc, ogc, gtc = _odd_in(hc, g1, mod_c[0], mod_c[1], win, gb, fm, **kw)
            gll, glc = _gate_layout(gtl, S // ML_CHUNK), _gate_layout(gtc, n_ctx // ML_CHUNK)
            hs = _mlstm(qc, kc, vc, glc, ql, kl, vl, gll, dk=dk, dv=dv)
            hg, wo = row2(mlstm_head_g[j]), odd_w_out[j].astype(BF16)
            hl = _odd_out(hs, ogl, hg, wo, hl, mod_l[2], dv=dv)
            if not last:
                raise NotImplementedError("context readout of an mLSTM layer is only needed when a layer follows it")

        nheads = peer_subkeys.shape[1]
        wqT = peer_w_q[i].T.astype(BF16)
        sub = peer_subkeys[i].reshape((2 * nheads,) + peer_subkeys.shape[3:]).astype(BF16)
        u, vT = peer_u[i].astype(BF16), peer_v[i].T.astype(BF16)
        nf = row2(norm_f_g)
        hl = _peer(hl, g2, mod_l[3], mod_l[4], mod_l[5], wqT, sub, u, vT, nf, final_norm=last)
        if not last:
            hc = _peer(hc, g2, mod_c[3], mod_c[4], mod_c[5], wqT, sub, u, vT, nf, final_norm=False)
    return hl
```

```python
import functools

import jax
import jax.numpy as jnp
from jax import lax
from jax.experimental import pallas as pl
from jax.experimental.pallas import tpu as pltpu

GRID_W = 64
EPS = 1e-6
ROPE_BASE = 10000.0
MLA_HEADS = 8
MLA_NOPE = 64
MLA_ROPE = 32
MLA_V = 64
ML_HEADS = 4
ML_CHUNK = 64
PEER_TOPK = 16

F32 = jnp.float32
BF16 = jnp.bfloat16
LANE = 128
VMEM_LIMIT = 56 * 1024 * 1024
NEG_INF = float("-inf")


def _cparams(*sem):
    return pltpu.CompilerParams(dimension_semantics=sem, vmem_limit_bytes=VMEM_LIMIT)


def _tile(n, pref):
    t = min(n, pref)
    assert n % t == 0, (n, pref)
    return t


def _rms(x, g):
    return x * lax.rsqrt(jnp.mean(x * x, axis=-1, keepdims=True) + EPS) * g


def _norm_mod(h_ref, g_ref, sh_ref, sc_ref):
    return _rms(h_ref[0], g_ref[...]) * (1.0 + sc_ref[0]) + sh_ref[0]


def _mm(a, b):
    return jnp.dot(a, b, preferred_element_type=F32)


def _mm_nt(a, b):
    return lax.dot_general(a, b, (((1,), (1,)), ((), ())), preferred_element_type=F32)


def _mod_kernel(c_ref, w_ref, b_ref, o_ref):
    c = c_ref[...]
    o_ref[...] = _mm((c * jax.nn.sigmoid(c)).astype(BF16), w_ref[...]) + b_ref[...]


def _mod_vectors(cc, w, b):
    R, D = cc.shape
    N = w.shape[1]
    tn = _tile(N, 1024)
    return pl.pallas_call(
        _mod_kernel, grid=(N // tn,),
        in_specs=[pl.BlockSpec((R, D), lambda j: (0, 0)),
                  pl.BlockSpec((D, tn), lambda j: (0, j)),
                  pl.BlockSpec((1, tn), lambda j: (0, j))],
        out_specs=pl.BlockSpec((R, tn), lambda j: (0, j)),
        out_shape=jax.ShapeDtypeStruct((R, N), F32),
        compiler_params=_cparams("parallel"), name="mod_vectors")(cc, w, b)


def _bvec(v):
    v3 = v[:, None, :]
    if v3.shape[0] == 1:
        return v3, (lambda b, *_: (0, 0, 0))
    return v3, (lambda b, *_: (b, 0, 0))


def _even_in_kernel(h_ref, g_ref, sh_ref, sc_ref, win_ref, qn_ref, kvn_ref, wq2_ref, wkv2_ref, cos_ref, sin_ref,
                    q_ref, k_ref, v_ref, bg_ref, z_ref, *, q_rank, kv_rank, conv_dim, q_scale):
    nl = _norm_mod(h_ref, g_ref, sh_ref, sc_ref)
    p = _mm(nl.astype(BF16), win_ref[...])
    o = q_rank + kv_rank
    pq, pc = p[:, :q_rank], p[:, q_rank:o]
    kr, krr = p[:, o:o + LANE], p[:, o + LANE:o + 2 * LANE]
    o += 2 * LANE
    bg, cg, u = p[:, o:o + conv_dim], p[:, o + conv_dim:o + 2 * conv_dim], p[:, o + 2 * conv_dim:o + 3 * conv_dim]
    cos, sin = cos_ref[...], sin_ref[...]
    hw = MLA_HEADS * LANE
    cos_t, sin_t = jnp.tile(cos, (1, MLA_HEADS)), jnp.tile(sin, (1, MLA_HEADS))
    qq = _mm(_rms(pq, qn_ref[...]).astype(BF16), wq2_ref[...])
    q_ref[0] = ((qq[:, :hw] * cos_t + qq[:, hw:] * sin_t) * q_scale).astype(BF16)
    kv = _mm(_rms(pc, kvn_ref[...]).astype(BF16), wkv2_ref[...])
    k_ref[0] = (kv[:, :hw] + jnp.tile(kr * cos + krr * sin, (1, MLA_HEADS))).astype(BF16)
    v_ref[0] = kv[:, hw:].astype(BF16)
    bg_ref[0] = bg
    z_ref[0] = cg * u


def _even_in(h, g, shift, scale, win, qn, kvn, wq2, wkv2, cos, sin, *, conv_dim, q_scale):
    B, T, D = h.shape
    tm = _tile(T, 512)
    hw = MLA_HEADS * LANE
    sh3, vmap_ = _bvec(shift)
    sc3, _ = _bvec(scale)
    const = lambda b, i: (0, 0)
    row = lambda b, i: (b, i, 0)
    kern = functools.partial(_even_in_kernel, q_rank=qn.shape[1], kv_rank=kvn.shape[1], conv_dim=conv_dim,
                             q_scale=q_scale)
    return pl.pallas_call(
        kern, grid=(B, T // tm),
        in_specs=[pl.BlockSpec((1, tm, D), row), pl.BlockSpec(g.shape, const),
                  pl.BlockSpec((1, 1, D), vmap_), pl.BlockSpec((1, 1, D), vmap_),
                  pl.BlockSpec(win.shape, const), pl.BlockSpec(qn.shape, const), pl.BlockSpec(kvn.shape, const),
                  pl.BlockSpec(wq2.shape, const), pl.BlockSpec(wkv2.shape, const),
                  pl.BlockSpec((tm, LANE), lambda b, i: (i, 0)), pl.BlockSpec((tm, LANE), lambda b, i: (i, 0))],
        out_specs=[pl.BlockSpec((1, tm, hw), row)] * 3 + [pl.BlockSpec((1, tm, conv_dim), row)] * 2,
        out_shape=[jax.ShapeDtypeStruct((B, T, hw), BF16)] * 3 + [jax.ShapeDtypeStruct((B, T, conv_dim), F32)] * 2,
        compiler_params=_cparams("parallel", "parallel"), name="even_in")(
            h, g, sh3, sc3, win, qn, kvn, wq2, wkv2, cos, sin)


def _attn_kernel(q_ref, *refs, nsets, hp):
    k_refs, v_refs, o_ref = refs[:nsets], refs[nsets:2 * nsets], refs[2 * nsets]
    heads = [slice(p * LANE, (p + 1) * LANE) for p in range(hp)]
    scores = [[_mm_nt(q_ref[0, :, lanes], k[0, :, lanes]) for k in k_refs] for lanes in heads]
    for lanes, ss in zip(heads, scores):
        m = functools.reduce(jnp.maximum, [jnp.max(s, axis=-1, keepdims=True) for s in ss])
        ps = [jnp.exp(s - m) for s in ss]
        l = functools.reduce(jnp.add, [jnp.sum(pr, axis=-1, keepdims=True) for pr in ps])
        o = functools.reduce(jnp.add, [_mm(pr.astype(BF16), v[0, :, lanes]) for pr, v in zip(ps, v_refs)])
        o_ref[0, :, lanes] = (o / l).astype(BF16)


def _attention(q, ks, vs):
    B, T, _ = q.shape
    tq = _tile(T, 256)
    hp = 4
    qmap = lambda b, h, i: (b, i, h)
    kmap = lambda b, h, i: (b, 0, h)
    kspecs = [pl.BlockSpec((1, k.shape[1], hp * LANE), kmap) for k in ks]
    return pl.pallas_call(
        functools.partial(_attn_kernel, nsets=len(ks), hp=hp), grid=(B, MLA_HEADS // hp, T // tq),
        in_specs=[pl.BlockSpec((1, tq, hp * LANE), qmap)] + kspecs + kspecs,
        out_specs=pl.BlockSpec((1, tq, hp * LANE), qmap),
        out_shape=jax.ShapeDtypeStruct(q.shape, BF16),
        compiler_params=_cparams("parallel", "parallel", "parallel"), name="mla_attention")(q, *ks, *vs)


def _even_out_kernel(a_ref, bg_ref, z_ref, zp_ref, zn_ref, h_ref, gate_ref, cw_ref, wa_ref, wc_ref, o_ref):
    i, last = pl.program_id(1), pl.num_programs(1) - 1
    z = z_ref[0]
    tm = z.shape[0]
    rows = lax.broadcasted_iota(jnp.int32, z.shape, 0)
    prev_row = zp_ref[0][7:8, :] * (i > 0).astype(F32)
    next_row = zn_ref[0][0:1, :] * (i < last).astype(F32)
    zm1 = jnp.where(rows == 0, prev_row, pltpu.roll(z, 1, 0))
    zp1 = jnp.where(rows == tm - 1, next_row, pltpu.roll(z, tm - 1, 0))
    cw = cw_ref[...]
    y = cw[0:1, :] * zm1 + cw[1:2, :] * z + cw[2:3, :] * zp1
    o = _mm(a_ref[0], wa_ref[...]) + _mm((bg_ref[0] * y).astype(BF16), wc_ref[...])
    o_ref[0] = h_ref[0] + gate_ref[0] * o


def _even_out(attn, bg, z, h, gate, cw, wa, wc):
    B, T, D = h.shape
    tm = _tile(T, 512)
    cd = z.shape[-1]
    nb8 = T // 8
    g3, gmap = _bvec(gate)
    row = lambda b, i: (b, i, 0)
    const = lambda b, i: (0, 0)
    return pl.pallas_call(
        _even_out_kernel, grid=(B, T // tm),
        in_specs=[pl.BlockSpec((1, tm, attn.shape[-1]), row), pl.BlockSpec((1, tm, cd), row),
                  pl.BlockSpec((1, tm, cd), row),
                  pl.BlockSpec((1, 8, cd), lambda b, i: (b, jnp.maximum(i * (tm // 8) - 1, 0), 0)),
                  pl.BlockSpec((1, 8, cd), lambda b, i: (b, jnp.minimum((i + 1) * (tm // 8), nb8 - 1), 0)),
                  pl.BlockSpec((1, tm, D), row), pl.BlockSpec((1, 1, D), gmap),
                  pl.BlockSpec(cw.shape, const), pl.BlockSpec(wa.shape, const), pl.BlockSpec(wc.shape, const)],
        out_specs=pl.BlockSpec((1, tm, D), row),
        out_shape=jax.ShapeDtypeStruct(h.shape, F32),
        compiler_params=_cparams("parallel", "parallel"), name="even_out")(
            attn, bg, z, z, z, h, g3, cw, wa, wc)


def _batcher_network(n):
    def merge(lo, hi, r):
        step = r * 2
        if step < hi - lo:
            yield from merge(lo, hi, step)
            yield from merge(lo + r, hi, step)
            yield from ((i, i + r) for i in range(lo + r, hi - r, step))
        else:
            yield (lo, lo + r)

    def sort(lo, hi):
        if hi - lo >= 1:
            mid = lo + (hi - lo) // 2
            yield from sort(lo, mid)
            yield from sort(mid + 1, hi)
            yield from merge(lo, hi, 1)

    assert n & (n - 1) == 0, n
    return list(sort(0, n - 1))


def _peer_topk_head(s1, s2, ab_scr):
    K = PEER_TOPK
    sub = 8
    assert s1.shape[0] == K * sub, s1.shape
    slabs = [s1[v * sub:(v + 1) * sub, :] for v in range(K)]
    for i, j in _batcher_network(K):
        slabs[i], slabs[j] = jnp.maximum(slabs[i], slabs[j]), jnp.minimum(slabs[i], slabs[j])
    for r in range(K):
        m = jnp.max(slabs[0], axis=0, keepdims=True)
        ab_scr[r:r + 1, :] = m
        hit = slabs[0] == m
        for v in range(K - 1 - r):
            slabs[v] = jnp.where(hit, slabs[v + 1], slabs[v])
    rank2 = jnp.full(s2.shape, float(K), F32)
    work = s2
    for r in range(K):
        m = jnp.max(work, axis=0, keepdims=True)
        ab_scr[K + r:K + r + 1, :] = m
        hit = work == m
        rank2 = jnp.where(hit, float(r), rank2)
        if r + 1 < K:
            work = jnp.where(hit, NEG_INF, work)
    a_rows = [ab_scr[i:i + 1, :] for i in range(K)]
    a_all = ab_scr[0:K, :]
    slabs = [a_all + ab_scr[K + j:K + j + 1, :] for j in range(K)]
    top = jnp.max(slabs[0], axis=0, keepdims=True)
    zsum = jnp.zeros_like(top)
    pops = jnp.zeros_like(a_all)
    for r in range(K):
        m = top if r == 0 else jnp.max(slabs[0], axis=0, keepdims=True)
        zsum = zsum + jnp.exp(m - top)
        hit = slabs[0] == m
        pops = pops + jnp.where(hit, 1.0, 0.0)
        for v in range(K - 1 - r):
            slabs[v] = jnp.where(hit, slabs[v + 1], slabs[v])
    cnt1 = jnp.zeros(s1.shape, F32)
    for i in range(K):
        cnt1 = jnp.where(s1 == a_rows[i], pops[i:i + 1, :], cnt1)
    c1 = jnp.where(s1 >= a_rows[K - 1], jnp.exp(s1 - a_rows[0]) / zsum, 0.0)
    e2 = jnp.exp(s2 - ab_scr[K:K + 1, :])
    return rank2, cnt1, c1, e2


def _peer_route_kernel(h_ref, g_ref, sh_ref, sc_ref, wqT_ref, sub_ref, xT_ref, r2_ref, e2_ref, n1_ref, c1_ref,
                       q_scr, s_scr, ab_scr, *, nheads):
    xT = _norm_mod(h_ref, g_ref, sh_ref, sc_ref).T.astype(BF16)
    xT_ref[0, 0] = xT
    q_scr[...] = _mm(wqT_ref[...], xT)
    dh = sub_ref.shape[2]
    for hp in range(2 * nheads):
        s_scr[hp] = _mm(sub_ref[hp], q_scr[hp * dh:(hp + 1) * dh, :].astype(BF16))

    def head_body(hh, carry):
        lw = ab_scr.shape[1]
        for l0 in range(0, s_scr.shape[2], lw):
            lanes = slice(l0, l0 + lw)
            rank2, cnt1, c1, e2 = _peer_topk_head(s_scr[2 * hh, :, lanes], s_scr[2 * hh + 1, :, lanes], ab_scr)
            r2_ref[0, 0, hh, :, lanes] = rank2.astype(BF16)
            e2_ref[0, 0, hh, :, lanes] = e2.astype(BF16)
            n1_ref[0, 0, hh, :, lanes] = cnt1
            c1_ref[0, 0, hh, :, lanes] = c1
        return carry

    lax.fori_loop(0, nheads, head_body, 0)


def _peer_gate_tiles(key0, nsub, g_scr, r2_ref, e2_ref, n1_ref, c1_ref, *, nheads, nkeys, sub_rows=16):
    tb = g_scr.shape[1]
    lane_w = min(tb, 2 * LANE)
    zero = jnp.zeros((sub_rows, lane_w), BF16)
    for l0 in range(0, tb, lane_w):
        lanes = slice(l0, l0 + lane_w)
        row_bf16 = lambda ref, hh, j: jnp.broadcast_to(ref[0, 0, hh, key0 + j:key0 + j + 1, lanes],
                                                       (sub_rows, lane_w)).astype(BF16)
        cnt = [[row_bf16(n1_ref, hh, j) for j in range(nsub)] for hh in range(nheads)]
        cc = [[row_bf16(c1_ref, hh, j) for j in range(nsub)] for hh in range(nheads)]
        for rg in range(nkeys // sub_rows):
            rows = slice(rg * sub_rows, (rg + 1) * sub_rows)
            accs = [None] * nsub
            for hh in range(nheads):
                r2t, e2t = r2_ref[0, 0, hh, rows, lanes], e2_ref[0, 0, hh, rows, lanes]
                for j in range(nsub):
                    t = jnp.where(r2t < cnt[hh][j], e2t, zero) * cc[hh][j]
                    accs[j] = t if accs[j] is None else accs[j] + t
            for j in range(nsub):
                lo = (key0 + j) * nkeys + rg * sub_rows
                g_scr[lo:lo + sub_rows, lanes] = accs[j]


def _peer_expert_kernel(h_ref, gate_ref, nf_ref, xT_ref, r2_ref, e2_ref, n1_ref, c1_ref, u_ref, vTp_ref, vTl_ref,
                        o_ref, acc_scr, g_scr, w_scr, *, nheads, nkeys, final_norm):
    c, nch = pl.program_id(2), pl.num_programs(2)

    @pl.when(c == 0)
    def _init():
        acc_scr[...] = jnp.zeros_like(acc_scr)
        w_scr[...] = jnp.zeros_like(w_scr)

    nsub = u_ref.shape[0] // nkeys
    group = min(nsub, 4)
    for k0 in range(0, nsub, group):
        _peer_gate_tiles(k0, group, g_scr, r2_ref, e2_ref, n1_ref, c1_ref, nheads=nheads, nkeys=nkeys)
    acc_scr[...] += _mm(vTp_ref[...], w_scr[...])
    act = _mm(u_ref[...], xT_ref[0, 0])
    ge = 0.5 * act * (1.0 + lax.erf(act * (2.0 ** -0.5)))
    w_scr[...] = g_scr[...] * ge.astype(BF16)

    @pl.when(c == nch - 1)
    def _epilogue():
        acc = acc_scr[...] + _mm(vTl_ref[...], w_scr[...])
        hn = h_ref[0] + gate_ref[0] * acc.T
        if final_norm:
            hn = _rms(hn, nf_ref[...])
        o_ref[0] = hn


def _peer(h, g, shift, scale, gate, wqT, sub, u, vT, nf, *, final_norm):
    B, T, D = h.shape
    Tb = _tile(T, 512)
    nexp = u.shape[0]
    nkeys = sub.shape[1]
    nheads = sub.shape[0] // 2
    ec = _tile(nexp, 1024)
    nst, nblk, nsub = nexp // ec, T // Tb, ec // nkeys
    sh3, vmap2 = _bvec(shift)
    sc3, _ = _bvec(scale)
    g3, vmap3 = _bvec(gate)

    blk2 = lambda b, i: (b, i, 0, 0, 0)
    hshape = (B, nblk, nheads, nkeys, Tb)
    hspec2 = pl.BlockSpec((1, 1, nheads, nkeys, Tb), blk2)
    xT, r2, e2, n1, c1 = pl.pallas_call(
        functools.partial(_peer_route_kernel, nheads=nheads), grid=(B, nblk),
        in_specs=[pl.BlockSpec((1, Tb, D), lambda b, i: (b, i, 0)), pl.BlockSpec(g.shape, lambda b, i: (0, 0)),
                  pl.BlockSpec((1, 1, D), vmap2), pl.BlockSpec((1, 1, D), vmap2),
                  pl.BlockSpec(wqT.shape, lambda b, i: (0, 0)), pl.BlockSpec(sub.shape, lambda b, i: (0, 0, 0))],
        out_specs=[pl.BlockSpec((1, 1, D, Tb), lambda b, i: (b, i, 0, 0))] + [hspec2] * 4,
        out_shape=[jax.ShapeDtypeStruct((B, nblk, D, Tb), BF16)]
        + [jax.ShapeDtypeStruct(hshape, dt) for dt in (BF16, BF16, F32, F32)],
        scratch_shapes=[pltpu.VMEM((wqT.shape[0], Tb), F32), pltpu.VMEM((2 * nheads, nkeys, Tb), F32),
                        pltpu.VMEM((2 * PEER_TOPK, LANE), F32)],
        compiler_params=_cparams("parallel", "parallel"), name="peer_route")(h, g, sh3, sc3, wqT, sub)

    row = lambda b, i, c: (b, i, 0)
    const2 = lambda b, i, c: (0, 0)
    blk3 = lambda b, i, c: (b, i, 0, 0, 0)
    hspec3 = pl.BlockSpec((1, 1, nheads, nkeys, Tb), blk3)
    cspec3 = pl.BlockSpec((1, 1, nheads, nsub, Tb), lambda b, i, c: (b, i, 0, c, 0))
    kern = functools.partial(_peer_expert_kernel, nheads=nheads, nkeys=nkeys, final_norm=final_norm)
    return pl.pallas_call(
        kern, grid=(B, nblk, nst),
        in_specs=[pl.BlockSpec((1, Tb, D), row), pl.BlockSpec((1, 1, D), vmap3), pl.BlockSpec(nf.shape, const2),
                  pl.BlockSpec((1, 1, D, Tb), lambda b, i, c: (b, i, 0, 0)), hspec3, hspec3, cspec3, cspec3,
                  pl.BlockSpec((ec, D), lambda b, i, c: (c, 0)),
                  pl.BlockSpec((D, ec), lambda b, i, c: (0, jnp.maximum(c - 1, 0))),
                  pl.BlockSpec((D, ec), lambda b, i, c: (0, jnp.where(c == nst - 1, nst - 1, 0)))],
        out_specs=pl.BlockSpec((1, Tb, D), row),
        out_shape=jax.ShapeDtypeStruct(h.shape, F32),
        scratch_shapes=[pltpu.VMEM((D, Tb), F32), pltpu.VMEM((ec, Tb), BF16), pltpu.VMEM((ec, Tb), BF16)],
        compiler_params=_cparams("parallel", "parallel", "arbitrary"), name="peer_experts")(
            h, g3, nf, xT, r2, e2, n1, c1, u, vT, vT)


def _odd_in_kernel(h_ref, g_ref, sh_ref, sc_ref, win_ref, gb_ref, fm_ref, q_ref, k_ref, v_ref, og_ref, gt_ref,
                   *, qk, vd, q_scale):
    nl = _norm_mod(h_ref, g_ref, sh_ref, sc_ref)
    p = _mm(nl.astype(BF16), win_ref[...])
    q_ref[0] = (p[:, :qk] * q_scale).astype(BF16)
    k_ref[0] = p[:, qk:2 * qk].astype(BF16)
    v_ref[0] = p[:, 2 * qk:2 * qk + vd].astype(BF16)
    og_ref[0] = jax.nn.sigmoid(p[:, 2 * qk + vd:2 * qk + 2 * vd])
    gt = p[:, 2 * qk + 2 * vd:] + gb_ref[...]
    log_sig = jnp.minimum(gt, 0.0) - jnp.log1p(jnp.exp(-jnp.abs(gt)))
    gt_ref[0] = jnp.where(fm_ref[...] > 0.5, log_sig, gt)


def _odd_in(h, g, shift, scale, win, gb, fm, *, qk, vd, q_scale):
    B, T, D = h.shape
    tm = _tile(T, 512)
    sh3, vmap_ = _bvec(shift)
    sc3, _ = _bvec(scale)
    const = lambda b, i: (0, 0)
    row = lambda b, i: (b, i, 0)
    widths = (qk, qk, vd, vd, LANE)
    dts = (BF16, BF16, BF16, F32, F32)
    return pl.pallas_call(
        functools.partial(_odd_in_kernel, qk=qk, vd=vd, q_scale=q_scale), grid=(B, T // tm),
        in_specs=[pl.BlockSpec((1, tm, D), row), pl.BlockSpec(g.shape, const),
                  pl.BlockSpec((1, 1, D), vmap_), pl.BlockSpec((1, 1, D), vmap_),
                  pl.BlockSpec(win.shape, const), pl.BlockSpec(gb.shape, const), pl.BlockSpec(fm.shape, const)],
        out_specs=[pl.BlockSpec((1, tm, w), row) for w in widths],
        out_shape=[jax.ShapeDtypeStruct((B, T, w), dt) for w, dt in zip(widths, dts)],
        compiler_params=_cparams("parallel", "parallel"), name="odd_in")(h, g, sh3, sc3, win, gb, fm)


def _bf16_terms(x):
    x1 = x.astype(BF16)
    r1 = x - x1.astype(F32)
    x2 = r1.astype(BF16)
    return x1, x2, (r1 - x2.astype(F32)).astype(BF16)


def _mlstm_chunks(qs, ks, vs, gls, states, dirs, with_out):
    G = range(len(dirs))
    L = qs[0].shape[0]
    r = lax.broadcasted_iota(jnp.int32, (L, L), 0)
    cidx = lax.broadcasted_iota(jnp.int32, (L, L), 1)
    eye = (r == cidx).astype(BF16)
    seen = {0: cidx <= r, 1: cidx >= r}
    seen_b = {d: seen[d].astype(BF16) for d in set(dirs)}
    seen_t = {d: seen[1 - d].astype(BF16) for d in set(dirs)}
    Cs, ns, ms = zip(*states)
    ir = [gls[g][2 * dirs[g]:2 * dirs[g] + 1, :] for g in G]
    frt = [_bf16_terms(jnp.broadcast_to(gls[g][2 * dirs[g] + 1:2 * dirs[g] + 2, :], (L, L))) for g in G]
    irt = [_bf16_terms(jnp.broadcast_to(ir[g], (L, L))) for g in G]
    add3 = lambda terms: terms[0] + terms[1] + terms[2]
    bcm = [add3([_mm_nt(seen_b[dirs[g]], t) for t in frt[g]]) for g in G]
    brm = [add3([_mm(t, seen_t[dirs[g]]) for t in frt[g]]) for g in G]
    icm = [add3([_mm_nt(eye, t) for t in irt[g]]) for g in G]
    bcol = [bcm[g][:, 0:1] for g in G]
    brow = [brm[g][0:1, :] for g in G]
    b_end = [bcol[g][L - 1:L, :] if dirs[g] == 0 else bcol[g][0:1, :] for g in G]
    a_row = [ir[g] + b_end[g] - brow[g] for g in G]
    a_col = [icm[g][:, 0:1] + b_end[g] - bcol[g] for g in G]
    m_new = [jnp.maximum(b_end[g] + ms[g], jnp.max(a_row[g], axis=1, keepdims=True)) for g in G]
    decay = [jnp.exp(b_end[g] + ms[g] - m_new[g]) for g in G]
    kw = [ks[g].astype(F32) * jnp.exp(a_col[g] - m_new[g]) for g in G]
    upd = [lax.dot_general(kw[g].astype(BF16), vs[g], (((0,), (0,)), ((), ())), preferred_element_type=F32)
           for g in G]
    new_states = [(decay[g] * Cs[g] + upd[g], decay[g] * ns[g] + jnp.sum(kw[g], axis=0, keepdims=True), m_new[g])
                  for g in G]
    if not with_out:
        return new_states, None
    dm = [jnp.where(seen[dirs[g]], bcm[g] - brm[g] + ir[g], NEG_INF) for g in G]
    m_t = [jnp.maximum(bcol[g] + ms[g], jnp.max(dm[g], axis=1, keepdims=True)) for g in G]
    inter = [jnp.exp(bcol[g] + ms[g] - m_t[g]) for g in G]
    sc = [_mm_nt(qs[g], ks[g]) * jnp.exp(dm[g] - m_t[g]) for g in G]
    intra = [_mm(sc[g].astype(BF16), vs[g]) for g in G]
    cross = [_mm(qs[g], Cs[g].astype(BF16)) for g in G]
    den = [jnp.sum(sc[g], axis=1, keepdims=True)
           + inter[g] * jnp.sum(qs[g].astype(F32) * ns[g], axis=1, keepdims=True) for g in G]
    outs = [(intra[g] + inter[g] * cross[g]) / jnp.maximum(jnp.abs(den[g]), jnp.exp(-m_t[g])) for g in G]
    return new_states, outs


def _mlstm_kernel(qc_ref, kc_ref, vc_ref, glc_ref, ql_ref, kl_ref, vl_ref, gll_ref, o_ref, *, hp, dk, dv):
    L = ML_CHUNK
    ncc, ncl = glc_ref.shape[2], gll_ref.shape[2]
    chains = [(p, d) for p in range(hp) for d in (0, 1)]

    def scan(q_ref, k_ref, v_ref, gl_ref, nc, lo, hi, states, out_mode):
        def body(j, sts):
            cjs = [j if d == 0 else nc - 1 - j for _, d in chains]
            sls = [pl.ds(pl.multiple_of(cj * L, L), L) for cj in cjs]
            qs = [q_ref[0, sl, p * dk:(p + 1) * dk] for (p, _), sl in zip(chains, sls)]
            ks = [k_ref[0, sl, p * dk:(p + 1) * dk] for (p, _), sl in zip(chains, sls)]
            vs = [v_ref[0, sl, p * dv:(p + 1) * dv] for (p, _), sl in zip(chains, sls)]
            gls = [gl_ref[0, p, cj] for (p, _), cj in zip(chains, cjs)]
            prev = [o_ref[0, sl, p * dv:(p + 1) * dv] for (p, _), sl in zip(chains, sls)] if out_mode == "add" else None
            new, outs = _mlstm_chunks(qs, ks, vs, gls, sts, [d for _, d in chains], out_mode is not None)
            if outs is not None:
                for g, ((p, _), sl) in enumerate(zip(chains, sls)):
                    tot = outs[g] if prev is None else prev[g].astype(F32) + outs[g]
                    o_ref[0, sl, p * dv:(p + 1) * dv] = tot.astype(o_ref.dtype)
            return tuple(new)

        return lax.fori_loop(lo, hi, body, states)

    assert ncl % 2 == 0, ncl
    st = tuple((jnp.zeros((dk, dv), F32), jnp.zeros((1, dk), F32), jnp.zeros((1, 1), F32)) for _ in chains)
    st = scan(qc_ref, kc_ref, vc_ref, glc_ref, ncc, 0, ncc, st, None)
    st = scan(ql_ref, kl_ref, vl_ref, gll_ref, ncl, 0, ncl // 2, st, "assign")
    scan(ql_ref, kl_ref, vl_ref, gll_ref, ncl, ncl // 2, ncl, st, "add")


def _mlstm(qc, kc, vc, glc, ql, kl, vl, gll, *, dk, dv):
    B, T, _ = ql.shape
    Tc = qc.shape[1]
    H = ML_HEADS
    hp = 4
    hmap = lambda b, h: (b, 0, h)
    gspec = lambda a: pl.BlockSpec((1, hp) + a.shape[2:], lambda b, h: (b, h, 0, 0, 0))
    return pl.pallas_call(
        functools.partial(_mlstm_kernel, hp=hp, dk=dk, dv=dv), grid=(B, H // hp),
        in_specs=[pl.BlockSpec((1, Tc, hp * dk), hmap), pl.BlockSpec((1, Tc, hp * dk), hmap),
                  pl.BlockSpec((1, Tc, hp * dv), hmap), gspec(glc),
                  pl.BlockSpec((1, T, hp * dk), hmap), pl.BlockSpec((1, T, hp * dk), hmap),
                  pl.BlockSpec((1, T, hp * dv), hmap), gspec(gll)],
        out_specs=pl.BlockSpec((1, T, hp * dv), hmap),
        out_shape=jax.ShapeDtypeStruct((B, T, H * dv), BF16),
        compiler_params=_cparams("parallel", "parallel"), name="mlstm")(qc, kc, vc, glc, ql, kl, vl, gll)


def _gate_layout(gt, nc):
    B = gt.shape[0]
    g = gt[:, :, :4 * ML_HEADS].reshape(B, nc, ML_CHUNK, 4, ML_HEADS)
    return g.transpose(0, 4, 1, 3, 2)


def _odd_out_kernel(hs_ref, og_ref, hg_ref, w_ref, h_ref, gate_ref, o_ref, *, dv):
    hs = hs_ref[0].astype(F32)
    parts = [hs[:, j * dv:(j + 1) * dv] for j in range(hs.shape[1] // dv)]
    normed = jnp.concatenate(
        [p * lax.rsqrt(jnp.mean(p * p, axis=-1, keepdims=True) + EPS) for p in parts], axis=1) * hg_ref[...]
    o_ref[0] = h_ref[0] + gate_ref[0] * _mm((og_ref[0] * normed).astype(BF16), w_ref[...])


def _odd_out(hs, og, hg, w, h, gate, *, dv):
    B, T, D = h.shape
    tm = _tile(T, 512)
    mix = hs.shape[-1]
    g3, gmap = _bvec(gate)
    row = lambda b, i: (b, i, 0)
    const = lambda b, i: (0, 0)
    return pl.pallas_call(
        functools.partial(_odd_out_kernel, dv=dv), grid=(B, T // tm),
        in_specs=[pl.BlockSpec((1, tm, mix), row), pl.BlockSpec((1, tm, mix), row), pl.BlockSpec(hg.shape, const),
                  pl.BlockSpec(w.shape, const), pl.BlockSpec((1, tm, D), row), pl.BlockSpec((1, 1, D), gmap)],
        out_specs=pl.BlockSpec((1, tm, D), row),
        out_shape=jax.ShapeDtypeStruct(h.shape, F32),
        compiler_params=_cparams("parallel", "parallel"), name="odd_out")(hs, og, hg, w, h, g3)


def _rope_tables(seq):
    n_freq = MLA_ROPE // 4
    inv = ROPE_BASE ** (-jnp.arange(n_freq, dtype=F32) / n_freq)
    t = jnp.arange(seq, dtype=jnp.int32)
    row = (t // GRID_W).astype(F32)
    col = (t % GRID_W).astype(F32)
    ang = jnp.concatenate([row[:, None] * inv, col[:, None] * inv], axis=-1)
    ones = jnp.ones((seq, MLA_NOPE), F32)
    tail = jnp.ones((seq, LANE - MLA_NOPE - MLA_ROPE), F32)
    cos = jnp.concatenate([ones, jnp.cos(ang), jnp.cos(ang), tail], axis=-1)
    sin = jnp.concatenate([0 * ones, jnp.sin(ang), jnp.sin(ang), 0 * tail], axis=-1)
    return cos, sin


def _rot_cols(w):
    half = w.shape[-1] // 2
    return jnp.concatenate([-w[..., half:], w[..., :half]], axis=-1)


def _slot(cols, width=LANE):
    pad = width - cols.shape[-1]
    out = jnp.pad(cols, [(0, 0)] * (cols.ndim - 1) + [(0, pad)])
    return out.reshape(out.shape[:-2] + (out.shape[-2] * width,))


def _even_weights(w_in, w_uq, w_ukv, w_out, q_rank, kv_rank):
    D = w_in.shape[0]
    H = MLA_HEADS
    o = q_rank + kv_rank
    w_kr = w_in[:, o:o + MLA_ROPE]
    lead = jnp.zeros((D, MLA_NOPE), F32)
    tail = jnp.zeros((D, LANE - MLA_NOPE - MLA_ROPE), F32)
    win = jnp.concatenate([w_in[:, :o], lead, w_kr, tail, lead, _rot_cols(w_kr), tail, w_in[:, o + MLA_ROPE:]],
                          axis=1).astype(BF16)
    uq = w_uq.reshape(q_rank, H, MLA_NOPE + MLA_ROPE)
    uq_rot = jnp.concatenate([jnp.zeros_like(uq[..., :MLA_NOPE]), _rot_cols(uq[..., MLA_NOPE:])], axis=-1)
    wq2 = jnp.concatenate([_slot(uq), _slot(uq_rot)], axis=1).astype(BF16)
    ukv = w_ukv.reshape(kv_rank, H, MLA_NOPE + MLA_V)
    wkv2 = jnp.concatenate([_slot(ukv[..., :MLA_NOPE]), _slot(ukv[..., MLA_NOPE:])], axis=1).astype(BF16)
    nattn = H * MLA_V
    wa = jnp.pad(w_out[:nattn].reshape(H, MLA_V, D), ((0, 0), (0, LANE - MLA_V), (0, 0))).reshape(H * LANE, D)
    return win, wq2, wkv2, wa.astype(BF16), w_out[nattn:].astype(BF16)


def kernel(x, c, ctx, c_ctx, norm1_g, norm2_g, w_mod, b_mod, even_w_in, mla_q_norm, mla_kv_norm, mla_w_uq, mla_w_ukv, conv_w, even_w_out, odd_w_in, mlstm_gate_b, mlstm_head_g, odd_w_out, peer_w_q, peer_subkeys, peer_u, peer_v, norm_f_g):
    B, S, D = x.shape
    n_ctx = ctx.shape[1]
    depth = norm1_g.shape[0]
    hl, hc = x, ctx
    row2 = lambda v: v[None, :]

    cc = jnp.zeros((-(-(B + 1) // 8) * 8, D), F32).at[:B].set(c).at[B].set(c_ctx)
    cos_l, sin_l = _rope_tables(S)
    cos_c = jnp.ones((n_ctx, LANE), F32)
    sin_c = jnp.zeros((n_ctx, LANE), F32)

    for i in range(depth):
        last = i == depth - 1
        j = i // 2
        mod = _mod_vectors(cc, w_mod[i].astype(BF16), row2(b_mod[i]))
        mod_l = [mod[:B, k * D:(k + 1) * D] for k in range(6)]
        mod_c = [mod[B:B + 1, k * D:(k + 1) * D] for k in range(6)]
        g1, g2 = row2(norm1_g[i]), row2(norm2_g[i])

        if i % 2 == 0:
            q_rank, kv_rank = mla_q_norm.shape[1], mla_kv_norm.shape[1]
            conv_dim = conv_w.shape[-1]
            win, wq2, wkv2, wa, wc = _even_weights(even_w_in[j], mla_w_uq[j], mla_w_ukv[j], even_w_out[j],
                                                   q_rank, kv_rank)
            qn, kvn = row2(mla_q_norm[j]), row2(mla_kv_norm[j])
            kw = dict(conv_dim=conv_dim, q_scale=float((MLA_NOPE + MLA_ROPE) ** -0.5))
            ql, kl, vl, bgl, zl = _even_in(hl, g1, mod_l[0], mod_l[1], win, qn, kvn, wq2, wkv2, cos_l, sin_l, **kw)
            qc, kc, vc, bgc, zc = _even_in(hc, g1, mod_c[0], mod_c[1], win, qn, kvn, wq2, wkv2, cos_c, sin_c, **kw)
            ol = _attention(ql, [kl, kc], [vl, vc])
            hl = _even_out(ol, bgl, zl, hl, mod_l[2], conv_w[j], wa, wc)
            if not last:
                oc = _attention(qc, [kc], [vc])
                hc = _even_out(oc, bgc, zc, hc, mod_c[2], conv_w[j], wa, wc)
        else:
            H = ML_HEADS
            mix = mlstm_head_g.shape[1]
            dv = mix // H
            qkvg = odd_w_in.shape[2] - mix
            dk = (qkvg - 4 * H - mix) // (2 * H)
            qk = H * dk
            w = odd_w_in[j]
            win = jnp.concatenate([w[:, :2 * qk + mix], w[:, qkvg:], w[:, 2 * qk + mix:qkvg],
                                   jnp.zeros((D, LANE - 4 * H), F32)], axis=1).astype(BF16)
            gb = jnp.pad(mlstm_gate_b[j], (0, LANE - 4 * H))[None, :]
            fm = jnp.pad(jnp.tile(jnp.repeat(jnp.array([0.0, 1.0], F32), H), 2), (0, LANE - 4 * H))[None, :]
            kw = dict(qk=qk, vd=mix, q_scale=float(dk ** -0.5))
            ql, kl, vl, ogl, gtl = _odd_in(hl, g1, mod_l[0], mod_l[1], win, gb, fm, **kw)
            qc, kc, vc, ogc, gtc = _odd_in(hc, g1, mod_c[0], mod_c[1], win, gb, fm, **kw)
            gll, glc = _gate_layout(gtl, S // ML_CHUNK), _gate_layout(gtc, n_ctx // ML_CHUNK)
            hs = _mlstm(qc, kc, vc, glc, ql, kl, vl, gll, dk=dk, dv=dv)
            hg, wo = row2(mlstm_head_g[j]), odd_w_out[j].astype(BF16)
            hl = _odd_out(hs, ogl, hg, wo, hl, mod_l[2], dv=dv)
            if not last:
                raise NotImplementedError("context readout of an mLSTM layer is only needed when a layer follows it")

        nheads = peer_subkeys.shape[1]
        wqT = peer_w_q[i].T.astype(BF16)
        sub = peer_subkeys[i].reshape((2 * nheads,) + peer_subkeys.shape[3:]).astype(BF16)
        u, vT = peer_u[i].astype(BF16), peer_v[i].T.astype(BF16)
        nf = row2(norm_f_g)
        hl = _peer(hl, g2, mod_l[3], mod_l[4], mod_l[5], wqT, sub, u, vT, nf, final_norm=last)
        if not last:
            hc = _peer(hc, g2, mod_c[3], mod_c[4], mod_c[5], wqT, sub, u, vT, nf, final_norm=False)
    return hl
```

```python
import functools

import jax
import jax.numpy as jnp
from jax import lax
from jax.experimental import pallas as pl
from jax.experimental.pallas import tpu as pltpu

GRID_W = 64
EPS = 1e-6
ROPE_BASE = 10000.0
MLA_HEADS = 8
MLA_NOPE = 64
MLA_ROPE = 32
MLA_V = 64
ML_HEADS = 4
ML_CHUNK = 64
PEER_TOPK = 16

F32 = jnp.float32
BF16 = jnp.bfloat16
LANE = 128
VMEM_LIMIT = 56 * 1024 * 1024
NEG_INF = float("-inf")


def _cparams(*sem):
    return pltpu.CompilerParams(dimension_semantics=sem, vmem_limit_bytes=VMEM_LIMIT)


def _tile(n, pref):
    t = min(n, pref)
    assert n % t == 0, (n, pref)
    return t


def _rms(x, g):
    return x * lax.rsqrt(jnp.mean(x * x, axis=-1, keepdims=True) + EPS) * g


def _norm_mod(h_ref, g_ref, sh_ref, sc_ref):
    return _rms(h_ref[0], g_ref[...]) * (1.0 + sc_ref[0]) + sh_ref[0]


def _mm(a, b):
    return jnp.dot(a, b, preferred_element_type=F32)


def _mm_nt(a, b):
    return lax.dot_general(a, b, (((1,), (1,)), ((), ())), preferred_element_type=F32)


def _mod_kernel(c_ref, w_ref, b_ref, o_ref):
    c = c_ref[...]
    o_ref[...] = _mm((c * jax.nn.sigmoid(c)).astype(BF16), w_ref[...]) + b_ref[...]


def _mod_vectors(cc, w, b):
    R, D = cc.shape
    N = w.shape[1]
    tn = _tile(N, 1024)
    return pl.pallas_call(
        _mod_kernel, grid=(N // tn,),
        in_specs=[pl.BlockSpec((R, D), lambda j: (0, 0)),
                  pl.BlockSpec((D, tn), lambda j: (0, j)),
                  pl.BlockSpec((1, tn), lambda j: (0, j))],
        out_specs=pl.BlockSpec((R, tn), lambda j: (0, j)),
        out_shape=jax.ShapeDtypeStruct((R, N), F32),
        compiler_params=_cparams("parallel"), name="mod_vectors")(cc, w, b)


def _bvec(v):
    v3 = v[:, None, :]
    if v3.shape[0] == 1:
        return v3, (lambda b, *_: (0, 0, 0))
    return v3, (lambda b, *_: (b, 0, 0))


def _even_in_kernel(h_ref, g_ref, sh_ref, sc_ref, win_ref, qn_ref, kvn_ref, wq2_ref, wkv2_ref, cos_ref, sin_ref,
                    q_ref, k_ref, v_ref, bg_ref, z_ref, *, q_rank, kv_rank, conv_dim, q_scale):
    nl = _norm_mod(h_ref, g_ref, sh_ref, sc_ref)
    p = _mm(nl.astype(BF16), win_ref[...])
    o = q_rank + kv_rank
    pq, pc = p[:, :q_rank], p[:, q_rank:o]
    kr, krr = p[:, o:o + LANE], p[:, o + LANE:o + 2 * LANE]
    o += 2 * LANE
    bg, cg, u = p[:, o:o + conv_dim], p[:, o + conv_dim:o + 2 * conv_dim], p[:, o + 2 * conv_dim:o + 3 * conv_dim]
    cos, sin = cos_ref[...], sin_ref[...]
    hw = MLA_HEADS * LANE
    cos_t, sin_t = jnp.tile(cos, (1, MLA_HEADS)), jnp.tile(sin, (1, MLA_HEADS))
    qq = _mm(_rms(pq, qn_ref[...]).astype(BF16), wq2_ref[...])
    q_ref[0] = ((qq[:, :hw] * cos_t + qq[:, hw:] * sin_t) * q_scale).astype(BF16)
    kv = _mm(_rms(pc, kvn_ref[...]).astype(BF16), wkv2_ref[...])
    k_ref[0] = (kv[:, :hw] + jnp.tile(kr * cos + krr * sin, (1, MLA_HEADS))).astype(BF16)
    v_ref[0] = kv[:, hw:].astype(BF16)
    bg_ref[0] = bg
    z_ref[0] = cg * u


def _even_in(h, g, shift, scale, win, qn, kvn, wq2, wkv2, cos, sin, *, conv_dim, q_scale):
    B, T, D = h.shape
    tm = _tile(T, 512)
    hw = MLA_HEADS * LANE
    sh3, vmap_ = _bvec(shift)
    sc3, _ = _bvec(scale)
    const = lambda b, i: (0, 0)
    row = lambda b, i: (b, i, 0)
    kern = functools.partial(_even_in_kernel, q_rank=qn.shape[1], kv_rank=kvn.shape[1], conv_dim=conv_dim,
                             q_scale=q_scale)
    return pl.pallas_call(
        kern, grid=(B, T // tm),
        in_specs=[pl.BlockSpec((1, tm, D), row), pl.BlockSpec(g.shape, const),
                  pl.BlockSpec((1, 1, D), vmap_), pl.BlockSpec((1, 1, D), vmap_),
                  pl.BlockSpec(win.shape, const), pl.BlockSpec(qn.shape, const), pl.BlockSpec(kvn.shape, const),
                  pl.BlockSpec(wq2.shape, const), pl.BlockSpec(wkv2.shape, const),
                  pl.BlockSpec((tm, LANE), lambda b, i: (i, 0)), pl.BlockSpec((tm, LANE), lambda b, i: (i, 0))],
        out_specs=[pl.BlockSpec((1, tm, hw), row)] * 3 + [pl.BlockSpec((1, tm, conv_dim), row)] * 2,
        out_shape=[jax.ShapeDtypeStruct((B, T, hw), BF16)] * 3 + [jax.ShapeDtypeStruct((B, T, conv_dim), F32)] * 2,
        compiler_params=_cparams("parallel", "parallel"), name="even_in")(
            h, g, sh3, sc3, win, qn, kvn, wq2, wkv2, cos, sin)


def _attn_kernel(q_ref, *refs, nsets, hp):
    k_refs, v_refs, o_ref = refs[:nsets], refs[nsets:2 * nsets], refs[2 * nsets]
    heads = [slice(p * LANE, (p + 1) * LANE) for p in range(hp)]
    scores = [[_mm_nt(q_ref[0, :, lanes], k[0, :, lanes]) for k in k_refs] for lanes in heads]
    for lanes, ss in zip(heads, scores):
        m = functools.reduce(jnp.maximum, [jnp.max(s, axis=-1, keepdims=True) for s in ss])
        ps = [jnp.exp(s - m) for s in ss]
        l = functools.reduce(jnp.add, [jnp.sum(pr, axis=-1, keepdims=True) for pr in ps])
        o = functools.reduce(jnp.add, [_mm(pr.astype(BF16), v[0, :, lanes]) for pr, v in zip(ps, v_refs)])
        o_ref[0, :, lanes] = (o / l).astype(BF16)


def _attention(q, ks, vs):
    B, T, _ = q.shape
    tq = _tile(T, 256)
    hp = 4
    qmap = lambda b, h, i: (b, i, h)
    kmap = lambda b, h, i: (b, 0, h)
    kspecs = [pl.BlockSpec((1, k.shape[1], hp * LANE), kmap) for k in ks]
    return pl.pallas_call(
        functools.partial(_attn_kernel, nsets=len(ks), hp=hp), grid=(B, MLA_HEADS // hp, T // tq),
        in_specs=[pl.BlockSpec((1, tq, hp * LANE), qmap)] + kspecs + kspecs,
        out_specs=pl.BlockSpec((1, tq, hp * LANE), qmap),
        out_shape=jax.ShapeDtypeStruct(q.shape, BF16),
        compiler_params=_cparams("parallel", "parallel", "parallel"), name="mla_attention")(q, *ks, *vs)


def _even_out_kernel(a_ref, bg_ref, z_ref, zp_ref, zn_ref, h_ref, gate_ref, cw_ref, wa_ref, wc_ref, o_ref):
    i, last = pl.program_id(1), pl.num_programs(1) - 1
    z = z_ref[0]
    tm = z.shape[0]
    rows = lax.broadcasted_iota(jnp.int32, z.shape, 0)
    prev_row = zp_ref[0][7:8, :] * (i > 0).astype(F32)
    next_row = zn_ref[0][0:1, :] * (i < last).astype(F32)
    zm1 = jnp.where(rows == 0, prev_row, pltpu.roll(z, 1, 0))
    zp1 = jnp.where(rows == tm - 1, next_row, pltpu.roll(z, tm - 1, 0))
    cw = cw_ref[...]
    y = cw[0:1, :] * zm1 + cw[1:2, :] * z + cw[2:3, :] * zp1
    o = _mm(a_ref[0], wa_ref[...]) + _mm((bg_ref[0] * y).astype(BF16), wc_ref[...])
    o_ref[0] = h_ref[0] + gate_ref[0] * o


def _even_out(attn, bg, z, h, gate, cw, wa, wc):
    B, T, D = h.shape
    tm = _tile(T, 512)
    cd = z.shape[-1]
    nb8 = T // 8
    g3, gmap = _bvec(gate)
    row = lambda b, i: (b, i, 0)
    const = lambda b, i: (0, 0)
    return pl.pallas_call(
        _even_out_kernel, grid=(B, T // tm),
        in_specs=[pl.BlockSpec((1, tm, attn.shape[-1]), row), pl.BlockSpec((1, tm, cd), row),
                  pl.BlockSpec((1, tm, cd), row),
                  pl.BlockSpec((1, 8, cd), lambda b, i: (b, jnp.maximum(i * (tm // 8) - 1, 0), 0)),
                  pl.BlockSpec((1, 8, cd), lambda b, i: (b, jnp.minimum((i + 1) * (tm // 8), nb8 - 1), 0)),
                  pl.BlockSpec((1, tm, D), row), pl.BlockSpec((1, 1, D), gmap),
                  pl.BlockSpec(cw.shape, const), pl.BlockSpec(wa.shape, const), pl.BlockSpec(wc.shape, const)],
        out_specs=pl.BlockSpec((1, tm, D), row),
        out_shape=jax.ShapeDtypeStruct(h.shape, F32),
        compiler_params=_cparams("parallel", "parallel"), name="even_out")(
            attn, bg, z, z, z, h, g3, cw, wa, wc)


def _batcher_network(n):
    def merge(lo, hi, r):
        step = r * 2
        if step < hi - lo:
            yield from merge(lo, hi, step)
            yield from merge(lo + r, hi, step)
            yield from ((i, i + r) for i in range(lo + r, hi - r, step))
        else:
            yield (lo, lo + r)

    def sort(lo, hi):
        if hi - lo >= 1:
            mid = lo + (hi - lo) // 2
            yield from sort(lo, mid)
            yield from sort(mid + 1, hi)
            yield from merge(lo, hi, 1)

    assert n & (n - 1) == 0, n
    return list(sort(0, n - 1))


def _peer_topk_head(s1, s2, ab_scr):
    K = PEER_TOPK
    sub = 8
    assert s1.shape[0] == K * sub, s1.shape
    slabs = [s1[v * sub:(v + 1) * sub, :] for v in range(K)]
    for i, j in _batcher_network(K):
        slabs[i], slabs[j] = jnp.maximum(slabs[i], slabs[j]), jnp.minimum(slabs[i], slabs[j])
    for r in range(K):
        m = jnp.max(slabs[0], axis=0, keepdims=True)
        ab_scr[r:r + 1, :] = m
        hit = slabs[0] == m
        for v in range(K - 1 - r):
            slabs[v] = jnp.where(hit, slabs[v + 1], slabs[v])
    rank2 = jnp.full(s2.shape, float(K), F32)
    work = s2
    for r in range(K):
        m = jnp.max(work, axis=0, keepdims=True)
        ab_scr[K + r:K + r + 1, :] = m
        hit = work == m
        rank2 = jnp.where(hit, float(r), rank2)
        if r + 1 < K:
            work = jnp.where(hit, NEG_INF, work)
    a_rows = [ab_scr[i:i + 1, :] for i in range(K)]
    a_all = ab_scr[0:K, :]
    slabs = [a_all + ab_scr[K + j:K + j + 1, :] for j in range(K)]
    top = jnp.max(slabs[0], axis=0, keepdims=True)
    zsum = jnp.zeros_like(top)
    pops = jnp.zeros_like(a_all)
    for r in range(K):
        m = top if r == 0 else jnp.max(slabs[0], axis=0, keepdims=True)
        zsum = zsum + jnp.exp(m - top)
        hit = slabs[0] == m
        pops = pops + jnp.where(hit, 1.0, 0.0)
        for v in range(K - 1 - r):
            slabs[v] = jnp.where(hit, slabs[v + 1], slabs[v])
    cnt1 = jnp.zeros(s1.shape, F32)
    for i in range(K):
        cnt1 = jnp.where(s1 == a_rows[i], pops[i:i + 1, :], cnt1)
    c1 = jnp.where(s1 >= a_rows[K - 1], jnp.exp(s1 - a_rows[0]) / zsum, 0.0)
    e2 = jnp.exp(s2 - ab_scr[K:K + 1, :])
    return rank2, cnt1, c1, e2


def _peer_route_kernel(h_ref, g_ref, sh_ref, sc_ref, wqT_ref, sub_ref, xT_ref, r2_ref, e2_ref, n1_ref, c1_ref,
                       q_scr, s_scr, ab_scr, *, nheads):
    xT = _norm_mod(h_ref, g_ref, sh_ref, sc_ref).T.astype(BF16)
    xT_ref[0, 0] = xT
    q_scr[...] = _mm(wqT_ref[...], xT)
    dh = sub_ref.shape[2]
    for hp in range(2 * nheads):
        s_scr[hp] = _mm(sub_ref[hp], q_scr[hp * dh:(hp + 1) * dh, :].astype(BF16))

    def head_body(hh, carry):
        lw = ab_scr.shape[1]
        for l0 in range(0, s_scr.shape[2], lw):
            lanes = slice(l0, l0 + lw)
            rank2, cnt1, c1, e2 = _peer_topk_head(s_scr[2 * hh, :, lanes], s_scr[2 * hh + 1, :, lanes], ab_scr)
            r2_ref[0, 0, hh, :, lanes] = rank2.astype(BF16)
            e2_ref[0, 0, hh, :, lanes] = e2.astype(BF16)
            n1_ref[0, 0, hh, :, lanes] = cnt1
            c1_ref[0, 0, hh, :, lanes] = c1
        return carry

    lax.fori_loop(0, nheads, head_body, 0)


def _peer_gate_tiles(key0, nsub, g_scr, r2_ref, e2_ref, n1_ref, c1_ref, *, nheads, nkeys, sub_rows=16):
    tb = g_scr.shape[1]
    lane_w = min(tb, 2 * LANE)
    zero = jnp.zeros((sub_rows, lane_w), BF16)
    for l0 in range(0, tb, lane_w):
        lanes = slice(l0, l0 + lane_w)
        row_bf16 = lambda ref, hh, j: jnp.broadcast_to(ref[0, 0, hh, key0 + j:key0 + j + 1, lanes],
                                                       (sub_rows, lane_w)).astype(BF16)
        cnt = [[row_bf16(n1_ref, hh, j) for j in range(nsub)] for hh in range(nheads)]
        cc = [[row_bf16(c1_ref, hh, j) for j in range(nsub)] for hh in range(nheads)]
        for rg in range(nkeys // sub_rows):
            rows = slice(rg * sub_rows, (rg + 1) * sub_rows)
            accs = [None] * nsub
            for hh in range(nheads):
                r2t, e2t = r2_ref[0, 0, hh, rows, lanes], e2_ref[0, 0, hh, rows, lanes]
                for j in range(nsub):
                    t = jnp.where(r2t < cnt[hh][j], e2t, zero) * cc[hh][j]
                    accs[j] = t if accs[j] is None else accs[j] + t
            for j in range(nsub):
                lo = (key0 + j) * nkeys + rg * sub_rows
                g_scr[lo:lo + sub_rows, lanes] = accs[j]


def _peer_expert_kernel(h_ref, gate_ref, nf_ref, xT_ref, r2_ref, e2_ref, n1_ref, c1_ref, u_ref, vTp_ref, vTl_ref,
                        o_ref, acc_scr, g_scr, w_scr, *, nheads, nkeys, final_norm):
    c, nch = pl.program_id(2), pl.num_programs(2)

    nsub = u_ref.shape[0] // nkeys
    group = min(nsub, 4)
    for k0 in range(0, nsub, group):
        _peer_gate_tiles(k0, group, g_scr, r2_ref, e2_ref, n1_ref, c1_ref, nheads=nheads, nkeys=nkeys)
    act = _mm(u_ref[...], xT_ref[0, 0])
    prev = _mm(vTp_ref[...], jnp.where(c > 0, w_scr[...], jnp.zeros_like(w_scr)))
    acc_scr[...] = jnp.where(c > 0, acc_scr[...] + prev, prev)
    ge = 0.5 * act * (1.0 + lax.erf(act * (2.0 ** -0.5)))
    w_scr[...] = g_scr[...] * ge.astype(BF16)

    @pl.when(c == nch - 1)
    def _epilogue():
        acc = acc_scr[...] + _mm(vTl_ref[...], w_scr[...])
        hn = h_ref[0] + gate_ref[0] * acc.T
        if final_norm:
            hn = _rms(hn, nf_ref[...])
        o_ref[0] = hn


def _peer(h, g, shift, scale, gate, wqT, sub, u, vT, nf, *, final_norm):
    B, T, D = h.shape
    Tb = _tile(T, 512)
    nexp = u.shape[0]
    nkeys = sub.shape[1]
    nheads = sub.shape[0] // 2
    ec = _tile(nexp, 1024)
    nst, nblk, nsub = nexp // ec, T // Tb, ec // nkeys
    sh3, vmap2 = _bvec(shift)
    sc3, _ = _bvec(scale)
    g3, vmap3 = _bvec(gate)

    blk2 = lambda b, i: (b, i, 0, 0, 0)
    hshape = (B, nblk, nheads, nkeys, Tb)
    hspec2 = pl.BlockSpec((1, 1, nheads, nkeys, Tb), blk2)
    xT, r2, e2, n1, c1 = pl.pallas_call(
        functools.partial(_peer_route_kernel, nheads=nheads), grid=(B, nblk),
        in_specs=[pl.BlockSpec((1, Tb, D), lambda b, i: (b, i, 0)), pl.BlockSpec(g.shape, lambda b, i: (0, 0)),
                  pl.BlockSpec((1, 1, D), vmap2), pl.BlockSpec((1, 1, D), vmap2),
                  pl.BlockSpec(wqT.shape, lambda b, i: (0, 0)), pl.BlockSpec(sub.shape, lambda b, i: (0, 0, 0))],
        out_specs=[pl.BlockSpec((1, 1, D, Tb), lambda b, i: (b, i, 0, 0))] + [hspec2] * 4,
        out_shape=[jax.ShapeDtypeStruct((B, nblk, D, Tb), BF16)]
        + [jax.ShapeDtypeStruct(hshape, dt) for dt in (BF16, BF16, F32, F32)],
        scratch_shapes=[pltpu.VMEM((wqT.shape[0], Tb), F32), pltpu.VMEM((2 * nheads, nkeys, Tb), F32),
                        pltpu.VMEM((2 * PEER_TOPK, LANE), F32)],
        compiler_params=_cparams("parallel", "parallel"), name="peer_route")(h, g, sh3, sc3, wqT, sub)

    row = lambda b, i, c: (b, i, 0)
    const2 = lambda b, i, c: (0, 0)
    blk3 = lambda b, i, c: (b, i, 0, 0, 0)
    hspec3 = pl.BlockSpec((1, 1, nheads, nkeys, Tb), blk3)
    cspec3 = pl.BlockSpec((1, 1, nheads, nsub, Tb), lambda b, i, c: (b, i, 0, c, 0))
    kern = functools.partial(_peer_expert_kernel, nheads=nheads, nkeys=nkeys, final_norm=final_norm)
    return pl.pallas_call(
        kern, grid=(B, nblk, nst),
        in_specs=[pl.BlockSpec((1, Tb, D), row), pl.BlockSpec((1, 1, D), vmap3), pl.BlockSpec(nf.shape, const2),
                  pl.BlockSpec((1, 1, D, Tb), lambda b, i, c: (b, i, 0, 0)), hspec3, hspec3, cspec3, cspec3,
                  pl.BlockSpec((ec, D), lambda b, i, c: (c, 0)),
                  pl.BlockSpec((D, ec), lambda b, i, c: (0, jnp.maximum(c - 1, 0))),
                  pl.BlockSpec((D, ec), lambda b, i, c: (0, jnp.where(c == nst - 1, nst - 1, 0)))],
        out_specs=pl.BlockSpec((1, Tb, D), row),
        out_shape=jax.ShapeDtypeStruct(h.shape, F32),
        scratch_shapes=[pltpu.VMEM((D, Tb), F32), pltpu.VMEM((ec, Tb), BF16), pltpu.VMEM((ec, Tb), BF16)],
        compiler_params=_cparams("parallel", "parallel", "arbitrary"), name="peer_experts")(
            h, g3, nf, xT, r2, e2, n1, c1, u, vT, vT)


def _odd_in_kernel(h_ref, g_ref, sh_ref, sc_ref, win_ref, gb_ref, fm_ref, q_ref, k_ref, v_ref, og_ref, gt_ref,
                   *, qk, vd, q_scale):
    nl = _norm_mod(h_ref, g_ref, sh_ref, sc_ref)
    p = _mm(nl.astype(BF16), win_ref[...])
    q_ref[0] = (p[:, :qk] * q_scale).astype(BF16)
    k_ref[0] = p[:, qk:2 * qk].astype(BF16)
    v_ref[0] = p[:, 2 * qk:2 * qk + vd].astype(BF16)
    og_ref[0] = jax.nn.sigmoid(p[:, 2 * qk + vd:2 * qk + 2 * vd])
    gt = p[:, 2 * qk + 2 * vd:] + gb_ref[...]
    log_sig = jnp.minimum(gt, 0.0) - jnp.log1p(jnp.exp(-jnp.abs(gt)))
    gt_ref[0] = jnp.where(fm_ref[...] > 0.5, log_sig, gt)


def _odd_in(h, g, shift, scale, win, gb, fm, *, qk, vd, q_scale):
    B, T, D = h.shape
    tm = _tile(T, 512)
    sh3, vmap_ = _bvec(shift)
    sc3, _ = _bvec(scale)
    const = lambda b, i: (0, 0)
    row = lambda b, i: (b, i, 0)
    widths = (qk, qk, vd, vd, LANE)
    dts = (BF16, BF16, BF16, F32, F32)
    return pl.pallas_call(
        functools.partial(_odd_in_kernel, qk=qk, vd=vd, q_scale=q_scale), grid=(B, T // tm),
        in_specs=[pl.BlockSpec((1, tm, D), row), pl.BlockSpec(g.shape, const),
                  pl.BlockSpec((1, 1, D), vmap_), pl.BlockSpec((1, 1, D), vmap_),
                  pl.BlockSpec(win.shape, const), pl.BlockSpec(gb.shape, const), pl.BlockSpec(fm.shape, const)],
        out_specs=[pl.BlockSpec((1, tm, w), row) for w in widths],
        out_shape=[jax.ShapeDtypeStruct((B, T, w), dt) for w, dt in zip(widths, dts)],
        compiler_params=_cparams("parallel", "parallel"), name="odd_in")(h, g, sh3, sc3, win, gb, fm)


def _bf16_terms(x):
    x1 = x.astype(BF16)
    r1 = x - x1.astype(F32)
    x2 = r1.astype(BF16)
    return x1, x2, (r1 - x2.astype(F32)).astype(BF16)


def _mlstm_chunks(qs, ks, vs, gls, states, dirs, with_out):
    G = range(len(dirs))
    L = qs[0].shape[0]
    r = lax.broadcasted_iota(jnp.int32, (L, L), 0)
    cidx = lax.broadcasted_iota(jnp.int32, (L, L), 1)
    eye = (r == cidx).astype(BF16)
    seen = {0: cidx <= r, 1: cidx >= r}
    seen_b = {d: seen[d].astype(BF16) for d in set(dirs)}
    seen_t = {d: seen[1 - d].astype(BF16) for d in set(dirs)}
    Cs, ns, ms = zip(*states)
    ir = [gls[g][2 * dirs[g]:2 * dirs[g] + 1, :] for g in G]
    frt = [_bf16_terms(jnp.broadcast_to(gls[g][2 * dirs[g] + 1:2 * dirs[g] + 2, :], (L, L))) for g in G]
    irt = [_bf16_terms(jnp.broadcast_to(ir[g], (L, L))) for g in G]
    add3 = lambda terms: terms[0] + terms[1] + terms[2]
    bcm = [add3([_mm_nt(seen_b[dirs[g]], t) for t in frt[g]]) for g in G]
    brm = [add3([_mm(t, seen_t[dirs[g]]) for t in frt[g]]) for g in G]
    icm = [add3([_mm_nt(eye, t) for t in irt[g]]) for g in G]
    bcol = [bcm[g][:, 0:1] for g in G]
    brow = [brm[g][0:1, :] for g in G]
    b_end = [bcol[g][L - 1:L, :] if dirs[g] == 0 else bcol[g][0:1, :] for g in G]
    a_row = [ir[g] + b_end[g] - brow[g] for g in G]
    a_col = [icm[g][:, 0:1] + b_end[g] - bcol[g] for g in G]
    m_new = [jnp.maximum(b_end[g] + ms[g], jnp.max(a_row[g], axis=1, keepdims=True)) for g in G]
    decay = [jnp.exp(b_end[g] + ms[g] - m_new[g]) for g in G]
    kw = [ks[g].astype(F32) * jnp.exp(a_col[g] - m_new[g]) for g in G]
    upd = [lax.dot_general(kw[g].astype(BF16), vs[g], (((0,), (0,)), ((), ())), preferred_element_type=F32)
           for g in G]
    new_states = [(decay[g] * Cs[g] + upd[g], decay[g] * ns[g] + jnp.sum(kw[g], axis=0, keepdims=True), m_new[g])
                  for g in G]
    if not with_out:
        return new_states, None
    dm = [jnp.where(seen[dirs[g]], bcm[g] - brm[g] + ir[g], NEG_INF) for g in G]
    m_t = [jnp.maximum(bcol[g] + ms[g], jnp.max(dm[g], axis=1, keepdims=True)) for g in G]
    inter = [jnp.exp(bcol[g] + ms[g] - m_t[g]) for g in G]
    sc = [_mm_nt(qs[g], ks[g]) * jnp.exp(dm[g] - m_t[g]) for g in G]
    intra = [_mm(sc[g].astype(BF16), vs[g]) for g in G]
    cross = [_mm(qs[g], Cs[g].astype(BF16)) for g in G]
    den = [jnp.sum(sc[g], axis=1, keepdims=True)
           + inter[g] * jnp.sum(qs[g].astype(F32) * ns[g], axis=1, keepdims=True) for g in G]
    outs = [(intra[g] + inter[g] * cross[g]) / jnp.maximum(jnp.abs(den[g]), jnp.exp(-m_t[g])) for g in G]
    return new_states, outs


def _mlstm_kernel(qc_ref, kc_ref, vc_ref, glc_ref, ql_ref, kl_ref, vl_ref, gll_ref, o_ref, *, hp, dk, dv):
    L = ML_CHUNK
    ncc, ncl = glc_ref.shape[2], gll_ref.shape[2]
    chains = [(p, d) for p in range(hp) for d in (0, 1)]

    def scan(q_ref, k_ref, v_ref, gl_ref, nc, lo, hi, states, out_mode):
        def body(j, sts):
            cjs = [j if d == 0 else nc - 1 - j for _, d in chains]
            sls = [pl.ds(pl.multiple_of(cj * L, L), L) for cj in cjs]
            qs = [q_ref[0, sl, p * dk:(p + 1) * dk] for (p, _), sl in zip(chains, sls)]
            ks = [k_ref[0, sl, p * dk:(p + 1) * dk] for (p, _), sl in zip(chains, sls)]
            vs = [v_ref[0, sl, p * dv:(p + 1) * dv] for (p, _), sl in zip(chains, sls)]
            gls = [gl_ref[0, p, cj] for (p, _), cj in zip(chains, cjs)]
            prev = [o_ref[0, sl, p * dv:(p + 1) * dv] for (p, _), sl in zip(chains, sls)] if out_mode == "add" else None
            new, outs = _mlstm_chunks(qs, ks, vs, gls, sts, [d for _, d in chains], out_mode is not None)
            if outs is not None:
                for g, ((p, _), sl) in enumerate(zip(chains, sls)):
                    tot = outs[g] if prev is None else prev[g].astype(F32) + outs[g]
                    o_ref[0, sl, p * dv:(p + 1) * dv] = tot.astype(o_ref.dtype)
            return tuple(new)

        return lax.fori_loop(lo, hi, body, states)

    assert ncl % 2 == 0, ncl
    st = tuple((jnp.zeros((dk, dv), F32), jnp.zeros((1, dk), F32), jnp.zeros((1, 1), F32)) for _ in chains)
    st = scan(qc_ref, kc_ref, vc_ref, glc_ref, ncc, 0, ncc, st, None)
    st = scan(ql_ref, kl_ref, vl_ref, gll_ref, ncl, 0, ncl // 2, st, "assign")
    scan(ql_ref, kl_ref, vl_ref, gll_ref, ncl, ncl // 2, ncl, st, "add")


def _mlstm(qc, kc, vc, glc, ql, kl, vl, gll, *, dk, dv):
    B, T, _ = ql.shape
    Tc = qc.shape[1]
    H = ML_HEADS
    hp = 4
    hmap = lambda b, h: (b, 0, h)
    gspec = lambda a: pl.BlockSpec((1, hp) + a.shape[2:], lambda b, h: (b, h, 0, 0, 0))
    return pl.pallas_call(
        functools.partial(_mlstm_kernel, hp=hp, dk=dk, dv=dv), grid=(B, H // hp),
        in_specs=[pl.BlockSpec((1, Tc, hp * dk), hmap), pl.BlockSpec((1, Tc, hp * dk), hmap),
                  pl.BlockSpec((1, Tc, hp * dv), hmap), gspec(glc),
                  pl.BlockSpec((1, T, hp * dk), hmap), pl.BlockSpec((1, T, hp * dk), hmap),
                  pl.BlockSpec((1, T, hp * dv), hmap), gspec(gll)],
        out_specs=pl.BlockSpec((1, T, hp * dv), hmap),
        out_shape=jax.ShapeDtypeStruct((B, T, H * dv), BF16),
        compiler_params=_cparams("parallel", "parallel"), name="mlstm")(qc, kc, vc, glc, ql, kl, vl, gll)


def _gate_layout(gt, nc):
    B = gt.shape[0]
    g = gt[:, :, :4 * ML_HEADS].reshape(B, nc, ML_CHUNK, 4, ML_HEADS)
    return g.transpose(0, 4, 1, 3, 2)


def _odd_out_kernel(hs_ref, og_ref, hg_ref, w_ref, h_ref, gate_ref, o_ref, *, dv):
    hs = hs_ref[0].astype(F32)
    parts = [hs[:, j * dv:(j + 1) * dv] for j in range(hs.shape[1] // dv)]
    normed = jnp.concatenate(
        [p * lax.rsqrt(jnp.mean(p * p, axis=-1, keepdims=True) + EPS) for p in parts], axis=1) * hg_ref[...]
    o_ref[0] = h_ref[0] + gate_ref[0] * _mm((og_ref[0] * normed).astype(BF16), w_ref[...])


def _odd_out(hs, og, hg, w, h, gate, *, dv):
    B, T, D = h.shape
    tm = _tile(T, 512)
    mix = hs.shape[-1]
    g3, gmap = _bvec(gate)
    row = lambda b, i: (b, i, 0)
    const = lambda b, i: (0, 0)
    return pl.pallas_call(
        functools.partial(_odd_out_kernel, dv=dv), grid=(B, T // tm),
        in_specs=[pl.BlockSpec((1, tm, mix), row), pl.BlockSpec((1, tm, mix), row), pl.BlockSpec(hg.shape, const),
                  pl.BlockSpec(w.shape, const), pl.BlockSpec((1, tm, D), row), pl.BlockSpec((1, 1, D), gmap)],
        out_specs=pl.BlockSpec((1, tm, D), row),
        out_shape=jax.ShapeDtypeStruct(h.shape, F32),
        compiler_params=_cparams("parallel", "parallel"), name="odd_out")(hs, og, hg, w, h, g3)


def _rope_tables(seq):
    n_freq = MLA_ROPE // 4
    inv = ROPE_BASE ** (-jnp.arange(n_freq, dtype=F32) / n_freq)
    t = jnp.arange(seq, dtype=jnp.int32)
    row = (t // GRID_W).astype(F32)
    col = (t % GRID_W).astype(F32)
    ang = jnp.concatenate([row[:, None] * inv, col[:, None] * inv], axis=-1)
    ones = jnp.ones((seq, MLA_NOPE), F32)
    tail = jnp.ones((seq, LANE - MLA_NOPE - MLA_ROPE), F32)
    cos = jnp.concatenate([ones, jnp.cos(ang), jnp.cos(ang), tail], axis=-1)
    sin = jnp.concatenate([0 * ones, jnp.sin(ang), jnp.sin(ang), 0 * tail], axis=-1)
    return cos, sin


def _rot_cols(w):
    half = w.shape[-1] // 2
    return jnp.concatenate([-w[..., half:], w[..., :half]], axis=-1)


def _slot(cols, width=LANE):
    pad = width - cols.shape[-1]
    out = jnp.pad(cols, [(0, 0)] * (cols.ndim - 1) + [(0, pad)])
    return out.reshape(out.shape[:-2] + (out.shape[-2] * width,))


def _even_weights(w_in, w_uq, w_ukv, w_out, q_rank, kv_rank):
    D = w_in.shape[0]
    H = MLA_HEADS
    o = q_rank + kv_rank
    w_kr = w_in[:, o:o + MLA_ROPE]
    lead = jnp.zeros((D, MLA_NOPE), F32)
    tail = jnp.zeros((D, LANE - MLA_NOPE - MLA_ROPE), F32)
    win = jnp.concatenate([w_in[:, :o], lead, w_kr, tail, lead, _rot_cols(w_kr), tail, w_in[:, o + MLA_ROPE:]],
                          axis=1).astype(BF16)
    uq = w_uq.reshape(q_rank, H, MLA_NOPE + MLA_ROPE)
    uq_rot = jnp.concatenate([jnp.zeros_like(uq[..., :MLA_NOPE]), _rot_cols(uq[..., MLA_NOPE:])], axis=-1)
    wq2 = jnp.concatenate([_slot(uq), _slot(uq_rot)], axis=1).astype(BF16)
    ukv = w_ukv.reshape(kv_rank, H, MLA_NOPE + MLA_V)
    wkv2 = jnp.concatenate([_slot(ukv[..., :MLA_NOPE]), _slot(ukv[..., MLA_NOPE:])], axis=1).astype(BF16)
    nattn = H * MLA_V
    wa = jnp.pad(w_out[:nattn].reshape(H, MLA_V, D), ((0, 0), (0, LANE - MLA_V), (0, 0))).reshape(H * LANE, D)
    return win, wq2, wkv2, wa.astype(BF16), w_out[nattn:].astype(BF16)


def kernel(x, c, ctx, c_ctx, norm1_g, norm2_g, w_mod, b_mod, even_w_in, mla_q_norm, mla_kv_norm, mla_w_uq, mla_w_ukv, conv_w, even_w_out, odd_w_in, mlstm_gate_b, mlstm_head_g, odd_w_out, peer_w_q, peer_subkeys, peer_u, peer_v, norm_f_g):
    B, S, D = x.shape
    n_ctx = ctx.shape[1]
    depth = norm1_g.shape[0]
    hl, hc = x, ctx
    row2 = lambda v: v[None, :]

    cc = jnp.zeros((-(-(B + 1) // 8) * 8, D), F32).at[:B].set(c).at[B].set(c_ctx)
    cos_l, sin_l = _rope_tables(S)
    cos_c = jnp.ones((n_ctx, LANE), F32)
    sin_c = jnp.zeros((n_ctx, LANE), F32)

    for i in range(depth):
        last = i == depth - 1
        j = i // 2
        mod = _mod_vectors(cc, w_mod[i].astype(BF16), row2(b_mod[i]))
        mod_l = [mod[:B, k * D:(k + 1) * D] for k in range(6)]
        mod_c = [mod[B:B + 1, k * D:(k + 1) * D] for k in range(6)]
        g1, g2 = row2(norm1_g[i]), row2(norm2_g[i])

        if i % 2 == 0:
            q_rank, kv_rank = mla_q_norm.shape[1], mla_kv_norm.shape[1]
            conv_dim = conv_w.shape[-1]
            win, wq2, wkv2, wa, wc = _even_weights(even_w_in[j], mla_w_uq[j], mla_w_ukv[j], even_w_out[j],
                                                   q_rank, kv_rank)
            qn, kvn = row2(mla_q_norm[j]), row2(mla_kv_norm[j])
            kw = dict(conv_dim=conv_dim, q_scale=float((MLA_NOPE + MLA_ROPE) ** -0.5))
            ql, kl, vl, bgl, zl = _even_in(hl, g1, mod_l[0], mod_l[1], win, qn, kvn, wq2, wkv2, cos_l, sin_l, **kw)
            qc, kc, vc, bgc, zc = _even_in(hc, g1, mod_c[0], mod_c[1], win, qn, kvn, wq2, wkv2, cos_c, sin_c, **kw)
            ol = _attention(ql, [kl, kc], [vl, vc])
            hl = _even_out(ol, bgl, zl, hl, mod_l[2], conv_w[j], wa, wc)
            if not last:
                oc = _attention(qc, [kc], [vc])
                hc = _even_out(oc, bgc, zc, hc, mod_c[2], conv_w[j], wa, wc)
        else:
            H = ML_HEADS
            mix = mlstm_head_g.shape[1]
            dv = mix // H
            qkvg = odd_w_in.shape[2] - mix
            dk = (qkvg - 4 * H - mix) // (2 * H)
            qk = H * dk
            w = odd_w_in[j]
            win = jnp.concatenate([w[:, :2 * qk + mix], w[:, qkvg:], w[:, 2 * qk + mix:qkvg],
                                   jnp.zeros((D, LANE - 4 * H), F32)], axis=1).astype(BF16)
            gb = jnp.pad(mlstm_gate_b[j], (0, LANE - 4 * H))[None, :]
            fm = jnp.pad(jnp.tile(jnp.repeat(jnp.array([0.0, 1.0], F32), H), 2), (0, LANE - 4 * H))[None, :]
            kw = dict(qk=qk, vd=mix, q_scale=float(dk ** -0.5))
            ql, kl, vl, ogl, gtl = _odd_in(hl, g1, mod_l[0], mod_l[1], win, gb, fm, **kw)
            qc, kc, vc, ogc, gtc = _odd_in(hc, g1, mod_c[0], mod_c[1], win, gb, fm, **kw)
            gll, glc = _gate_layout(gtl, S // ML_CHUNK), _gate_layout(gtc, n_ctx // ML_CHUNK)
            hs = _mlstm(qc, kc, vc, glc, ql, kl, vl, gll, dk=dk, dv=dv)
            hg, wo = row2(mlstm_head_g[j]), odd_w_out[j].astype(BF16)
            hl = _odd_out(hs, ogl, hg, wo, hl, mod_l[2], dv=dv)
            if not last:
                raise NotImplementedError("context readout of an mLSTM layer is only needed when a layer follows it")

        nheads = peer_subkeys.shape[1]
        wqT = peer_w_q[i].T.astype(BF16)
        sub = peer_subkeys[i].reshape((2 * nheads,) + peer_subkeys.shape[3:]).astype(BF16)
        u, vT = peer_u[i].astype(BF16), peer_v[i].T.astype(BF16)
        nf = row2(norm_f_g)
        hl = _peer(hl, g2, mod_l[3], mod_l[4], mod_l[5], wqT, sub, u, vT, nf, final_norm=last)
        if not last:
            hc = _peer(hc, g2, mod_c[3], mod_c[4], mod_c[5], wqT, sub, u, vT, nf, final_norm=False)
    return hl
```

```python
import functools

import jax
import jax.numpy as jnp
from jax import lax
from jax.experimental import pallas as pl
from jax.experimental.pallas import tpu as pltpu

GRID_W = 64
EPS = 1e-6
ROPE_BASE = 10000.0
MLA_HEADS = 8
MLA_NOPE = 64
MLA_ROPE = 32
MLA_V = 64
ML_HEADS = 4
ML_CHUNK = 64
PEER_TOPK = 16

F32 = jnp.float32
BF16 = jnp.bfloat16
LANE = 128
VMEM_LIMIT = 56 * 1024 * 1024
NEG_INF = float("-inf")


def _cparams(*sem):
    return pltpu.CompilerParams(dimension_semantics=sem, vmem_limit_bytes=VMEM_LIMIT)


def _tile(n, pref):
    t = min(n, pref)
    assert n % t == 0, (n, pref)
    return t


def _rms(x, g):
    return x * lax.rsqrt(jnp.mean(x * x, axis=-1, keepdims=True) + EPS) * g


def _norm_mod(h_ref, g_ref, sh_ref, sc_ref):
    return _rms(h_ref[0], g_ref[...]) * (1.0 + sc_ref[0]) + sh_ref[0]


def _mm(a, b):
    return jnp.dot(a, b, preferred_element_type=F32)


def _mm_nt(a, b):
    return lax.dot_general(a, b, (((1,), (1,)), ((), ())), preferred_element_type=F32)


def _mod_kernel(c_ref, w_ref, b_ref, o_ref):
    c = c_ref[...]
    o_ref[...] = _mm((c * jax.nn.sigmoid(c)).astype(BF16), w_ref[...]) + b_ref[...]


def _mod_vectors(cc, w, b):
    R, D = cc.shape
    N = w.shape[1]
    tn = _tile(N, 1024)
    return pl.pallas_call(
        _mod_kernel, grid=(N // tn,),
        in_specs=[pl.BlockSpec((R, D), lambda j: (0, 0)),
                  pl.BlockSpec((D, tn), lambda j: (0, j)),
                  pl.BlockSpec((1, tn), lambda j: (0, j))],
        out_specs=pl.BlockSpec((R, tn), lambda j: (0, j)),
        out_shape=jax.ShapeDtypeStruct((R, N), F32),
        compiler_params=_cparams("parallel"), name="mod_vectors")(cc, w, b)


def _bvec(v):
    v3 = v[:, None, :]
    if v3.shape[0] == 1:
        return v3, (lambda b, *_: (0, 0, 0))
    return v3, (lambda b, *_: (b, 0, 0))


def _even_in_kernel(h_ref, g_ref, sh_ref, sc_ref, win_ref, qn_ref, kvn_ref, wq2_ref, wkv2_ref, cos_ref, sin_ref,
                    q_ref, k_ref, v_ref, bg_ref, z_ref, *, q_rank, kv_rank, conv_dim, q_scale):
    nl = _norm_mod(h_ref, g_ref, sh_ref, sc_ref)
    p = _mm(nl.astype(BF16), win_ref[...])
    o = q_rank + kv_rank
    pq, pc = p[:, :q_rank], p[:, q_rank:o]
    kr, krr = p[:, o:o + LANE], p[:, o + LANE:o + 2 * LANE]
    o += 2 * LANE
    bg, cg, u = p[:, o:o + conv_dim], p[:, o + conv_dim:o + 2 * conv_dim], p[:, o + 2 * conv_dim:o + 3 * conv_dim]
    cos, sin = cos_ref[...], sin_ref[...]
    hw = MLA_HEADS * LANE
    cos_t, sin_t = jnp.tile(cos, (1, MLA_HEADS)), jnp.tile(sin, (1, MLA_HEADS))
    qq = _mm(_rms(pq, qn_ref[...]).astype(BF16), wq2_ref[...])
    q_ref[0] = ((qq[:, :hw] * cos_t + qq[:, hw:] * sin_t) * q_scale).astype(BF16)
    kv = _mm(_rms(pc, kvn_ref[...]).astype(BF16), wkv2_ref[...])
    k_ref[0] = (kv[:, :hw] + jnp.tile(kr * cos + krr * sin, (1, MLA_HEADS))).astype(BF16)
    v_ref[0] = kv[:, hw:].astype(BF16)
    bg_ref[0] = bg
    z_ref[0] = cg * u


def _even_in(h, g, shift, scale, win, qn, kvn, wq2, wkv2, cos, sin, *, conv_dim, q_scale):
    B, T, D = h.shape
    tm = _tile(T, 512)
    hw = MLA_HEADS * LANE
    sh3, vmap_ = _bvec(shift)
    sc3, _ = _bvec(scale)
    const = lambda b, i: (0, 0)
    row = lambda b, i: (b, i, 0)
    kern = functools.partial(_even_in_kernel, q_rank=qn.shape[1], kv_rank=kvn.shape[1], conv_dim=conv_dim,
                             q_scale=q_scale)
    return pl.pallas_call(
        kern, grid=(B, T // tm),
        in_specs=[pl.BlockSpec((1, tm, D), row), pl.BlockSpec(g.shape, const),
                  pl.BlockSpec((1, 1, D), vmap_), pl.BlockSpec((1, 1, D), vmap_),
                  pl.BlockSpec(win.shape, const), pl.BlockSpec(qn.shape, const), pl.BlockSpec(kvn.shape, const),
                  pl.BlockSpec(wq2.shape, const), pl.BlockSpec(wkv2.shape, const),
                  pl.BlockSpec((tm, LANE), lambda b, i: (i, 0)), pl.BlockSpec((tm, LANE), lambda b, i: (i, 0))],
        out_specs=[pl.BlockSpec((1, tm, hw), row)] * 3 + [pl.BlockSpec((1, tm, conv_dim), row)] * 2,
        out_shape=[jax.ShapeDtypeStruct((B, T, hw), BF16)] * 3 + [jax.ShapeDtypeStruct((B, T, conv_dim), F32)] * 2,
        compiler_params=_cparams("parallel", "parallel"), name="even_in")(
            h, g, sh3, sc3, win, qn, kvn, wq2, wkv2, cos, sin)


def _attn_kernel(q_ref, *refs, nsets, hp):
    k_refs, v_refs, o_ref = refs[:nsets], refs[nsets:2 * nsets], refs[2 * nsets]
    heads = [slice(p * LANE, (p + 1) * LANE) for p in range(hp)]
    scores = [[_mm_nt(q_ref[0, :, lanes], k[0, :, lanes]) for k in k_refs] for lanes in heads]
    for lanes, ss in zip(heads, scores):
        m = functools.reduce(jnp.maximum, [jnp.max(s, axis=-1, keepdims=True) for s in ss])
        ps = [jnp.exp(s - m) for s in ss]
        l = functools.reduce(jnp.add, [jnp.sum(pr, axis=-1, keepdims=True) for pr in ps])
        o = functools.reduce(jnp.add, [_mm(pr.astype(BF16), v[0, :, lanes]) for pr, v in zip(ps, v_refs)])
        o_ref[0, :, lanes] = (o / l).astype(BF16)


def _attention(q, ks, vs):
    B, T, _ = q.shape
    tq = _tile(T, 256)
    hp = 4
    qmap = lambda b, h, i: (b, i, h)
    kmap = lambda b, h, i: (b, 0, h)
    kspecs = [pl.BlockSpec((1, k.shape[1], hp * LANE), kmap) for k in ks]
    return pl.pallas_call(
        functools.partial(_attn_kernel, nsets=len(ks), hp=hp), grid=(B, MLA_HEADS // hp, T // tq),
        in_specs=[pl.BlockSpec((1, tq, hp * LANE), qmap)] + kspecs + kspecs,
        out_specs=pl.BlockSpec((1, tq, hp * LANE), qmap),
        out_shape=jax.ShapeDtypeStruct(q.shape, BF16),
        compiler_params=_cparams("parallel", "parallel", "parallel"), name="mla_attention")(q, *ks, *vs)


def _even_out_kernel(a_ref, bg_ref, z_ref, zp_ref, zn_ref, h_ref, gate_ref, cw_ref, wa_ref, wc_ref, o_ref):
    i, last = pl.program_id(1), pl.num_programs(1) - 1
    z = z_ref[0]
    tm = z.shape[0]
    rows = lax.broadcasted_iota(jnp.int32, z.shape, 0)
    prev_row = zp_ref[0][7:8, :] * (i > 0).astype(F32)
    next_row = zn_ref[0][0:1, :] * (i < last).astype(F32)
    zm1 = jnp.where(rows == 0, prev_row, pltpu.roll(z, 1, 0))
    zp1 = jnp.where(rows == tm - 1, next_row, pltpu.roll(z, tm - 1, 0))
    cw = cw_ref[...]
    y = cw[0:1, :] * zm1 + cw[1:2, :] * z + cw[2:3, :] * zp1
    o = _mm(a_ref[0], wa_ref[...]) + _mm((bg_ref[0] * y).astype(BF16), wc_ref[...])
    o_ref[0] = h_ref[0] + gate_ref[0] * o


def _even_out(attn, bg, z, h, gate, cw, wa, wc):
    B, T, D = h.shape
    tm = _tile(T, 512)
    cd = z.shape[-1]
    nb8 = T // 8
    g3, gmap = _bvec(gate)
    row = lambda b, i: (b, i, 0)
    const = lambda b, i: (0, 0)
    return pl.pallas_call(
        _even_out_kernel, grid=(B, T // tm),
        in_specs=[pl.BlockSpec((1, tm, attn.shape[-1]), row), pl.BlockSpec((1, tm, cd), row),
                  pl.BlockSpec((1, tm, cd), row),
                  pl.BlockSpec((1, 8, cd), lambda b, i: (b, jnp.maximum(i * (tm // 8) - 1, 0), 0)),
                  pl.BlockSpec((1, 8, cd), lambda b, i: (b, jnp.minimum((i + 1) * (tm // 8), nb8 - 1), 0)),
                  pl.BlockSpec((1, tm, D), row), pl.BlockSpec((1, 1, D), gmap),
                  pl.BlockSpec(cw.shape, const), pl.BlockSpec(wa.shape, const), pl.BlockSpec(wc.shape, const)],
        out_specs=pl.BlockSpec((1, tm, D), row),
        out_shape=jax.ShapeDtypeStruct(h.shape, F32),
        compiler_params=_cparams("parallel", "parallel"), name="even_out")(
            attn, bg, z, z, z, h, g3, cw, wa, wc)


def _batcher_network(n):
    def merge(lo, hi, r):
        step = r * 2
        if step < hi - lo:
            yield from merge(lo, hi, step)
            yield from merge(lo + r, hi, step)
            yield from ((i, i + r) for i in range(lo + r, hi - r, step))
        else:
            yield (lo, lo + r)

    def sort(lo, hi):
        if hi - lo >= 1:
            mid = lo + (hi - lo) // 2
            yield from sort(lo, mid)
            yield from sort(mid + 1, hi)
            yield from merge(lo, hi, 1)

    assert n & (n - 1) == 0, n
    return list(sort(0, n - 1))


def _peer_topk_head(s1, s2, ab_scr):
    K = PEER_TOPK
    sub = 8
    assert s1.shape[0] == K * sub, s1.shape
    slabs = [s1[v * sub:(v + 1) * sub, :] for v in range(K)]
    for i, j in _batcher_network(K):
        slabs[i], slabs[j] = jnp.maximum(slabs[i], slabs[j]), jnp.minimum(slabs[i], slabs[j])
    for r in range(K):
        m = jnp.max(slabs[0], axis=0, keepdims=True)
        ab_scr[r:r + 1, :] = m
        hit = slabs[0] == m
        for v in range(K - 1 - r):
            slabs[v] = jnp.where(hit, slabs[v + 1], slabs[v])
    rank2 = jnp.full(s2.shape, float(K), F32)
    work = s2
    for r in range(K):
        m = jnp.max(work, axis=0, keepdims=True)
        ab_scr[K + r:K + r + 1, :] = m
        hit = work == m
        rank2 = jnp.where(hit, float(r), rank2)
        if r + 1 < K:
            work = jnp.where(hit, NEG_INF, work)
    a_rows = [ab_scr[i:i + 1, :] for i in range(K)]
    a_all = ab_scr[0:K, :]
    slabs = [a_all + ab_scr[K + j:K + j + 1, :] for j in range(K)]
    top = jnp.max(slabs[0], axis=0, keepdims=True)
    zsum = jnp.zeros_like(top)
    pops = jnp.zeros_like(a_all)
    for r in range(K):
        m = top if r == 0 else jnp.max(slabs[0], axis=0, keepdims=True)
        zsum = zsum + jnp.exp(m - top)
        hit = slabs[0] == m
        pops = pops + jnp.where(hit, 1.0, 0.0)
        for v in range(K - 1 - r):
            slabs[v] = jnp.where(hit, slabs[v + 1], slabs[v])
    cnt1 = jnp.zeros(s1.shape, F32)
    for i in range(K):
        cnt1 = jnp.where(s1 == a_rows[i], pops[i:i + 1, :], cnt1)
    c1 = jnp.where(s1 >= a_rows[K - 1], jnp.exp(s1 - a_rows[0]) / zsum, 0.0)
    e2 = jnp.exp(s2 - ab_scr[K:K + 1, :])
    return rank2, cnt1, c1, e2


def _peer_route_kernel(h_ref, g_ref, sh_ref, sc_ref, wqT_ref, sub_ref, xT_ref, r2_ref, e2_ref, n1_ref, c1_ref,
                       q_scr, s_scr, ab_scr, *, nheads):
    xT = _norm_mod(h_ref, g_ref, sh_ref, sc_ref).T.astype(BF16)
    xT_ref[0, 0] = xT
    q_scr[...] = _mm(wqT_ref[...], xT)
    dh = sub_ref.shape[2]
    for hp in range(2 * nheads):
        s_scr[hp] = _mm(sub_ref[hp], q_scr[hp * dh:(hp + 1) * dh, :].astype(BF16))

    def head_body(hh, carry):
        lw = ab_scr.shape[1]
        for l0 in range(0, s_scr.shape[2], lw):
            lanes = slice(l0, l0 + lw)
            rank2, cnt1, c1, e2 = _peer_topk_head(s_scr[2 * hh, :, lanes], s_scr[2 * hh + 1, :, lanes], ab_scr)
            r2_ref[0, 0, hh, :, lanes] = rank2.astype(BF16)
            e2_ref[0, 0, hh, :, lanes] = e2.astype(BF16)
            n1_ref[0, 0, hh, :, lanes] = cnt1
            c1_ref[0, 0, hh, :, lanes] = c1
        return carry

    lax.fori_loop(0, nheads, head_body, 0)


def _peer_gate_tiles(key0, nsub, g_scr, r2_ref, e2_ref, n1_ref, c1_ref, *, nheads, nkeys, sub_rows=16):
    tb = g_scr.shape[1]
    lane_w = min(tb, 2 * LANE)
    zero = jnp.zeros((sub_rows, lane_w), BF16)
    for l0 in range(0, tb, lane_w):
        lanes = slice(l0, l0 + lane_w)
        row_bf16 = lambda ref, hh, j: jnp.broadcast_to(ref[0, 0, hh, key0 + j:key0 + j + 1, lanes],
                                                       (sub_rows, lane_w)).astype(BF16)
        cnt = [[row_bf16(n1_ref, hh, j) for j in range(nsub)] for hh in range(nheads)]
        cc = [[row_bf16(c1_ref, hh, j) for j in range(nsub)] for hh in range(nheads)]
        for rg in range(nkeys // sub_rows):
            rows = slice(rg * sub_rows, (rg + 1) * sub_rows)
            accs = [None] * nsub
            for hh in range(nheads):
                r2t, e2t = r2_ref[0, 0, hh, rows, lanes], e2_ref[0, 0, hh, rows, lanes]
                for j in range(nsub):
                    t = jnp.where(r2t < cnt[hh][j], e2t, zero) * cc[hh][j]
                    accs[j] = t if accs[j] is None else accs[j] + t
            for j in range(nsub):
                lo = (key0 + j) * nkeys + rg * sub_rows
                g_scr[lo:lo + sub_rows, lanes] = accs[j]


def _peer_expert_kernel(h_ref, gate_ref, nf_ref, xT_ref, r2_ref, e2_ref, n1_ref, c1_ref, u_ref, vTp_ref, vTl_ref,
                        o_ref, acc_scr, g_scr, w_scr, *, nheads, nkeys, final_norm):
    c, nch = pl.program_id(2), pl.num_programs(2)

    @pl.when(c == 0)
    def _init():
        acc_scr[...] = jnp.zeros_like(acc_scr)
        w_scr[...] = jnp.zeros_like(w_scr)

    nsub = u_ref.shape[0] // nkeys
    group = min(nsub, 8)
    for k0 in range(0, nsub, group):
        _peer_gate_tiles(k0, group, g_scr, r2_ref, e2_ref, n1_ref, c1_ref, nheads=nheads, nkeys=nkeys)
    acc_scr[...] += _mm(vTp_ref[...], w_scr[...])
    act = _mm(u_ref[...], xT_ref[0, 0])
    ge = 0.5 * act * (1.0 + lax.erf(act * (2.0 ** -0.5)))
    w_scr[...] = g_scr[...] * ge.astype(BF16)

    @pl.when(c == nch - 1)
    def _epilogue():
        acc = acc_scr[...] + _mm(vTl_ref[...], w_scr[...])
        hn = h_ref[0] + gate_ref[0] * acc.T
        if final_norm:
            hn = _rms(hn, nf_ref[...])
        o_ref[0] = hn


def _peer(h, g, shift, scale, gate, wqT, sub, u, vT, nf, *, final_norm):
    B, T, D = h.shape
    Tb = _tile(T, 512)
    nexp = u.shape[0]
    nkeys = sub.shape[1]
    nheads = sub.shape[0] // 2
    ec = _tile(nexp, 1024)
    nst, nblk, nsub = nexp // ec, T // Tb, ec // nkeys
    sh3, vmap2 = _bvec(shift)
    sc3, _ = _bvec(scale)
    g3, vmap3 = _bvec(gate)

    blk2 = lambda b, i: (b, i, 0, 0, 0)
    hshape = (B, nblk, nheads, nkeys, Tb)
    hspec2 = pl.BlockSpec((1, 1, nheads, nkeys, Tb), blk2)
    xT, r2, e2, n1, c1 = pl.pallas_call(
        functools.partial(_peer_route_kernel, nheads=nheads), grid=(B, nblk),
        in_specs=[pl.BlockSpec((1, Tb, D), lambda b, i: (b, i, 0)), pl.BlockSpec(g.shape, lambda b, i: (0, 0)),
                  pl.BlockSpec((1, 1, D), vmap2), pl.BlockSpec((1, 1, D), vmap2),
                  pl.BlockSpec(wqT.shape, lambda b, i: (0, 0)), pl.BlockSpec(sub.shape, lambda b, i: (0, 0, 0))],
        out_specs=[pl.BlockSpec((1, 1, D, Tb), lambda b, i: (b, i, 0, 0))] + [hspec2] * 4,
        out_shape=[jax.ShapeDtypeStruct((B, nblk, D, Tb), BF16)]
        + [jax.ShapeDtypeStruct(hshape, dt) for dt in (BF16, BF16, F32, F32)],
        scratch_shapes=[pltpu.VMEM((wqT.shape[0], Tb), F32), pltpu.VMEM((2 * nheads, nkeys, Tb), F32),
                        pltpu.VMEM((2 * PEER_TOPK, LANE), F32)],
        compiler_params=_cparams("parallel", "parallel"), name="peer_route")(h, g, sh3, sc3, wqT, sub)

    row = lambda b, i, c: (b, i, 0)
    const2 = lambda b, i, c: (0, 0)
    blk3 = lambda b, i, c: (b, i, 0, 0, 0)
    hspec3 = pl.BlockSpec((1, 1, nheads, nkeys, Tb), blk3)
    cspec3 = pl.BlockSpec((1, 1, nheads, nsub, Tb), lambda b, i, c: (b, i, 0, c, 0))
    kern = functools.partial(_peer_expert_kernel, nheads=nheads, nkeys=nkeys, final_norm=final_norm)
    return pl.pallas_call(
        kern, grid=(B, nblk, nst),
        in_specs=[pl.BlockSpec((1, Tb, D), row), pl.BlockSpec((1, 1, D), vmap3), pl.BlockSpec(nf.shape, const2),
                  pl.BlockSpec((1, 1, D, Tb), lambda b, i, c: (b, i, 0, 0)), hspec3, hspec3, cspec3, cspec3,
                  pl.BlockSpec((ec, D), lambda b, i, c: (c, 0)),
                  pl.BlockSpec((D, ec), lambda b, i, c: (0, jnp.maximum(c - 1, 0))),
                  pl.BlockSpec((D, ec), lambda b, i, c: (0, jnp.where(c == nst - 1, nst - 1, 0)))],
        out_specs=pl.BlockSpec((1, Tb, D), row),
        out_shape=jax.ShapeDtypeStruct(h.shape, F32),
        scratch_shapes=[pltpu.VMEM((D, Tb), F32), pltpu.VMEM((ec, Tb), BF16), pltpu.VMEM((ec, Tb), BF16)],
        compiler_params=_cparams("parallel", "parallel", "arbitrary"), name="peer_experts")(
            h, g3, nf, xT, r2, e2, n1, c1, u, vT, vT)


def _odd_in_kernel(h_ref, g_ref, sh_ref, sc_ref, win_ref, gb_ref, fm_ref, q_ref, k_ref, v_ref, og_ref, gt_ref,
                   *, qk, vd, q_scale):
    nl = _norm_mod(h_ref, g_ref, sh_ref, sc_ref)
    p = _mm(nl.astype(BF16), win_ref[...])
    q_ref[0] = (p[:, :qk] * q_scale).astype(BF16)
    k_ref[0] = p[:, qk:2 * qk].astype(BF16)
    v_ref[0] = p[:, 2 * qk:2 * qk + vd].astype(BF16)
    og_ref[0] = jax.nn.sigmoid(p[:, 2 * qk + vd:2 * qk + 2 * vd])
    gt = p[:, 2 * qk + 2 * vd:] + gb_ref[...]
    log_sig = jnp.minimum(gt, 0.0) - jnp.log1p(jnp.exp(-jnp.abs(gt)))
    gt_ref[0] = jnp.where(fm_ref[...] > 0.5, log_sig, gt)


def _odd_in(h, g, shift, scale, win, gb, fm, *, qk, vd, q_scale):
    B, T, D = h.shape
    tm = _tile(T, 512)
    sh3, vmap_ = _bvec(shift)
    sc3, _ = _bvec(scale)
    const = lambda b, i: (0, 0)
    row = lambda b, i: (b, i, 0)
    widths = (qk, qk, vd, vd, LANE)
    dts = (BF16, BF16, BF16, F32, F32)
    return pl.pallas_call(
        functools.partial(_odd_in_kernel, qk=qk, vd=vd, q_scale=q_scale), grid=(B, T // tm),
        in_specs=[pl.BlockSpec((1, tm, D), row), pl.BlockSpec(g.shape, const),
                  pl.BlockSpec((1, 1, D), vmap_), pl.BlockSpec((1, 1, D), vmap_),
                  pl.BlockSpec(win.shape, const), pl.BlockSpec(gb.shape, const), pl.BlockSpec(fm.shape, const)],
        out_specs=[pl.BlockSpec((1, tm, w), row) for w in widths],
        out_shape=[jax.ShapeDtypeStruct((B, T, w), dt) for w, dt in zip(widths, dts)],
        compiler_params=_cparams("parallel", "parallel"), name="odd_in")(h, g, sh3, sc3, win, gb, fm)


def _bf16_terms(x):
    x1 = x.astype(BF16)
    r1 = x - x1.astype(F32)
    x2 = r1.astype(BF16)
    return x1, x2, (r1 - x2.astype(F32)).astype(BF16)


def _mlstm_chunks(qs, ks, vs, gls, states, dirs, with_out):
    G = range(len(dirs))
    L = qs[0].shape[0]
    r = lax.broadcasted_iota(jnp.int32, (L, L), 0)
    cidx = lax.broadcasted_iota(jnp.int32, (L, L), 1)
    eye = (r == cidx).astype(BF16)
    seen = {0: cidx <= r, 1: cidx >= r}
    seen_b = {d: seen[d].astype(BF16) for d in set(dirs)}
    seen_t = {d: seen[1 - d].astype(BF16) for d in set(dirs)}
    Cs, ns, ms = zip(*states)
    ir = [gls[g][2 * dirs[g]:2 * dirs[g] + 1, :] for g in G]
    frt = [_bf16_terms(jnp.broadcast_to(gls[g][2 * dirs[g] + 1:2 * dirs[g] + 2, :], (L, L))) for g in G]
    irt = [_bf16_terms(jnp.broadcast_to(ir[g], (L, L))) for g in G]
    add3 = lambda terms: terms[0] + terms[1] + terms[2]
    bcm = [add3([_mm_nt(seen_b[dirs[g]], t) for t in frt[g]]) for g in G]
    brm = [add3([_mm(t, seen_t[dirs[g]]) for t in frt[g]]) for g in G]
    icm = [add3([_mm_nt(eye, t) for t in irt[g]]) for g in G]
    bcol = [bcm[g][:, 0:1] for g in G]
    brow = [brm[g][0:1, :] for g in G]
    b_end = [bcol[g][L - 1:L, :] if dirs[g] == 0 else bcol[g][0:1, :] for g in G]
    a_row = [ir[g] + b_end[g] - brow[g] for g in G]
    a_col = [icm[g][:, 0:1] + b_end[g] - bcol[g] for g in G]
    m_new = [jnp.maximum(b_end[g] + ms[g], jnp.max(a_row[g], axis=1, keepdims=True)) for g in G]
    decay = [jnp.exp(b_end[g] + ms[g] - m_new[g]) for g in G]
    kw = [ks[g].astype(F32) * jnp.exp(a_col[g] - m_new[g]) for g in G]
    upd = [lax.dot_general(kw[g].astype(BF16), vs[g], (((0,), (0,)), ((), ())), preferred_element_type=F32)
           for g in G]
    new_states = [(decay[g] * Cs[g] + upd[g], decay[g] * ns[g] + jnp.sum(kw[g], axis=0, keepdims=True), m_new[g])
                  for g in G]
    if not with_out:
        return new_states, None
    dm = [jnp.where(seen[dirs[g]], bcm[g] - brm[g] + ir[g], NEG_INF) for g in G]
    m_t = [jnp.maximum(bcol[g] + ms[g], jnp.max(dm[g], axis=1, keepdims=True)) for g in G]
    inter = [jnp.exp(bcol[g] + ms[g] - m_t[g]) for g in G]
    sc = [_mm_nt(qs[g], ks[g]) * jnp.exp(dm[g] - m_t[g]) for g in G]
    intra = [_mm(sc[g].astype(BF16), vs[g]) for g in G]
    cross = [_mm(qs[g], Cs[g].astype(BF16)) for g in G]
    den = [jnp.sum(sc[g], axis=1, keepdims=True)
           + inter[g] * jnp.sum(qs[g].astype(F32) * ns[g], axis=1, keepdims=True) for g in G]
    outs = [(intra[g] + inter[g] * cross[g]) / jnp.maximum(jnp.abs(den[g]), jnp.exp(-m_t[g])) for g in G]
    return new_states, outs


def _mlstm_kernel(qc_ref, kc_ref, vc_ref, glc_ref, ql_ref, kl_ref, vl_ref, gll_ref, o_ref, *, hp, dk, dv):
    L = ML_CHUNK
    ncc, ncl = glc_ref.shape[2], gll_ref.shape[2]
    chains = [(p, d) for p in range(hp) for d in (0, 1)]

    def scan(q_ref, k_ref, v_ref, gl_ref, nc, lo, hi, states, out_mode):
        def body(j, sts):
            cjs = [j if d == 0 else nc - 1 - j for _, d in chains]
            sls = [pl.ds(pl.multiple_of(cj * L, L), L) for cj in cjs]
            qs = [q_ref[0, sl, p * dk:(p + 1) * dk] for (p, _), sl in zip(chains, sls)]
            ks = [k_ref[0, sl, p * dk:(p + 1) * dk] for (p, _), sl in zip(chains, sls)]
            vs = [v_ref[0, sl, p * dv:(p + 1) * dv] for (p, _), sl in zip(chains, sls)]
            gls = [gl_ref[0, p, cj] for (p, _), cj in zip(chains, cjs)]
            prev = [o_ref[0, sl, p * dv:(p + 1) * dv] for (p, _), sl in zip(chains, sls)] if out_mode == "add" else None
            new, outs = _mlstm_chunks(qs, ks, vs, gls, sts, [d for _, d in chains], out_mode is not None)
            if outs is not None:
                for g, ((p, _), sl) in enumerate(zip(chains, sls)):
                    tot = outs[g] if prev is None else prev[g].astype(F32) + outs[g]
                    o_ref[0, sl, p * dv:(p + 1) * dv] = tot.astype(o_ref.dtype)
            return tuple(new)

        return lax.fori_loop(lo, hi, body, states)

    assert ncl % 2 == 0, ncl
    st = tuple((jnp.zeros((dk, dv), F32), jnp.zeros((1, dk), F32), jnp.zeros((1, 1), F32)) for _ in chains)
    st = scan(qc_ref, kc_ref, vc_ref, glc_ref, ncc, 0, ncc, st, None)
    st = scan(ql_ref, kl_ref, vl_ref, gll_ref, ncl, 0, ncl // 2, st, "assign")
    scan(ql_ref, kl_ref, vl_ref, gll_ref, ncl, ncl // 2, ncl, st, "add")


def _mlstm(qc, kc, vc, glc, ql, kl, vl, gll, *, dk, dv):
    B, T, _ = ql.shape
    Tc = qc.shape[1]
    H = ML_HEADS
    hp = 4
    hmap = lambda b, h: (b, 0, h)
    gspec = lambda a: pl.BlockSpec((1, hp) + a.shape[2:], lambda b, h: (b, h, 0, 0, 0))
    return pl.pallas_call(
        functools.partial(_mlstm_kernel, hp=hp, dk=dk, dv=dv), grid=(B, H // hp),
        in_specs=[pl.BlockSpec((1, Tc, hp * dk), hmap), pl.BlockSpec((1, Tc, hp * dk), hmap),
                  pl.BlockSpec((1, Tc, hp * dv), hmap), gspec(glc),
                  pl.BlockSpec((1, T, hp * dk), hmap), pl.BlockSpec((1, T, hp * dk), hmap),
                  pl.BlockSpec((1, T, hp * dv), hmap), gspec(gll)],
        out_specs=pl.BlockSpec((1, T, hp * dv), hmap),
        out_shape=jax.ShapeDtypeStruct((B, T, H * dv), BF16),
        compiler_params=_cparams("parallel", "parallel"), name="mlstm")(qc, kc, vc, glc, ql, kl, vl, gll)


def _gate_layout(gt, nc):
    B = gt.shape[0]
    g = gt[:, :, :4 * ML_HEADS].reshape(B, nc, ML_CHUNK, 4, ML_HEADS)
    return g.transpose(0, 4, 1, 3, 2)


def _odd_out_kernel(hs_ref, og_ref, hg_ref, w_ref, h_ref, gate_ref, o_ref, *, dv):
    hs = hs_ref[0].astype(F32)
    parts = [hs[:, j * dv:(j + 1) * dv] for j in range(hs.shape[1] // dv)]
    normed = jnp.concatenate(
        [p * lax.rsqrt(jnp.mean(p * p, axis=-1, keepdims=True) + EPS) for p in parts], axis=1) * hg_ref[...]
    o_ref[0] = h_ref[0] + gate_ref[0] * _mm((og_ref[0] * normed).astype(BF16), w_ref[...])


def _odd_out(hs, og, hg, w, h, gate, *, dv):
    B, T, D = h.shape
    tm = _tile(T, 512)
    mix = hs.shape[-1]
    g3, gmap = _bvec(gate)
    row = lambda b, i: (b, i, 0)
    const = lambda b, i: (0, 0)
    return pl.pallas_call(
        functools.partial(_odd_out_kernel, dv=dv), grid=(B, T // tm),
        in_specs=[pl.BlockSpec((1, tm, mix), row), pl.BlockSpec((1, tm, mix), row), pl.BlockSpec(hg.shape, const),
                  pl.BlockSpec(w.shape, const), pl.BlockSpec((1, tm, D), row), pl.BlockSpec((1, 1, D), gmap)],
        out_specs=pl.BlockSpec((1, tm, D), row),
        out_shape=jax.ShapeDtypeStruct(h.shape, F32),
        compiler_params=_cparams("parallel", "parallel"), name="odd_out")(hs, og, hg, w, h, g3)


def _rope_tables(seq):
    n_freq = MLA_ROPE // 4
    inv = ROPE_BASE ** (-jnp.arange(n_freq, dtype=F32) / n_freq)
    t = jnp.arange(seq, dtype=jnp.int32)
    row = (t // GRID_W).astype(F32)
    col = (t % GRID_W).astype(F32)
    ang = jnp.concatenate([row[:, None] * inv, col[:, None] * inv], axis=-1)
    ones = jnp.ones((seq, MLA_NOPE), F32)
    tail = jnp.ones((seq, LANE - MLA_NOPE - MLA_ROPE), F32)
    cos = jnp.concatenate([ones, jnp.cos(ang), jnp.cos(ang), tail], axis=-1)
    sin = jnp.concatenate([0 * ones, jnp.sin(ang), jnp.sin(ang), 0 * tail], axis=-1)
    return cos, sin


def _rot_cols(w):
    half = w.shape[-1] // 2
    return jnp.concatenate([-w[..., half:], w[..., :half]], axis=-1)


def _slot(cols, width=LANE):
    pad = width - cols.shape[-1]
    out = jnp.pad(cols, [(0, 0)] * (cols.ndim - 1) + [(0, pad)])
    return out.reshape(out.shape[:-2] + (out.shape[-2] * width,))


def _even_weights(w_in, w_uq, w_ukv, w_out, q_rank, kv_rank):
    D = w_in.shape[0]
    H = MLA_HEADS
    o = q_rank + kv_rank
    w_kr = w_in[:, o:o + MLA_ROPE]
    lead = jnp.zeros((D, MLA_NOPE), F32)
    tail = jnp.zeros((D, LANE - MLA_NOPE - MLA_ROPE), F32)
    win = jnp.concatenate([w_in[:, :o], lead, w_kr, tail, lead, _rot_cols(w_kr), tail, w_in[:, o + MLA_ROPE:]],
                          axis=1).astype(BF16)
    uq = w_uq.reshape(q_rank, H, MLA_NOPE + MLA_ROPE)
    uq_rot = jnp.concatenate([jnp.zeros_like(uq[..., :MLA_NOPE]), _rot_cols(uq[..., MLA_NOPE:])], axis=-1)
    wq2 = jnp.concatenate([_slot(uq), _slot(uq_rot)], axis=1).astype(BF16)
    ukv = w_ukv.reshape(kv_rank, H, MLA_NOPE + MLA_V)
    wkv2 = jnp.concatenate([_slot(ukv[..., :MLA_NOPE]), _slot(ukv[..., MLA_NOPE:])], axis=1).astype(BF16)
    nattn = H * MLA_V
    wa = jnp.pad(w_out[:nattn].reshape(H, MLA_V, D), ((0, 0), (0, LANE - MLA_V), (0, 0))).reshape(H * LANE, D)
    return win, wq2, wkv2, wa.astype(BF16), w_out[nattn:].astype(BF16)


def kernel(x, c, ctx, c_ctx, norm1_g, norm2_g, w_mod, b_mod, even_w_in, mla_q_norm, mla_kv_norm, mla_w_uq, mla_w_ukv, conv_w, even_w_out, odd_w_in, mlstm_gate_b, mlstm_head_g, odd_w_out, peer_w_q, peer_subkeys, peer_u, peer_v, norm_f_g):
    B, S, D = x.shape
    n_ctx = ctx.shape[1]
    depth = norm1_g.shape[0]
    hl, hc = x, ctx
    row2 = lambda v: v[None, :]

    cc = jnp.zeros((-(-(B + 1) // 8) * 8, D), F32).at[:B].set(c).at[B].set(c_ctx)
    cos_l, sin_l = _rope_tables(S)
    cos_c = jnp.ones((n_ctx, LANE), F32)
    sin_c = jnp.zeros((n_ctx, LANE), F32)

    for i in range(depth):
        last = i == depth - 1
        j = i // 2
        mod = _mod_vectors(cc, w_mod[i].astype(BF16), row2(b_mod[i]))
        mod_l = [mod[:B, k * D:(k + 1) * D] for k in range(6)]
        mod_c = [mod[B:B + 1, k * D:(k + 1) * D] for k in range(6)]
        g1, g2 = row2(norm1_g[i]), row2(norm2_g[i])

        if i % 2 == 0:
            q_rank, kv_rank = mla_q_norm.shape[1], mla_kv_norm.shape[1]
            conv_dim = conv_w.shape[-1]
            win, wq2, wkv2, wa, wc = _even_weights(even_w_in[j], mla_w_uq[j], mla_w_ukv[j], even_w_out[j],
                                                   q_rank, kv_rank)
            qn, kvn = row2(mla_q_norm[j]), row2(mla_kv_norm[j])
            kw = dict(conv_dim=conv_dim, q_scale=float((MLA_NOPE + MLA_ROPE) ** -0.5))
            ql, kl, vl, bgl, zl = _even_in(hl, g1, mod_l[0], mod_l[1], win, qn, kvn, wq2, wkv2, cos_l, sin_l, **kw)
            qc, kc, vc, bgc, zc = _even_in(hc, g1, mod_c[0], mod_c[1], win, qn, kvn, wq2, wkv2, cos_c, sin_c, **kw)
            ol = _attention(ql, [kl, kc], [vl, vc])
            hl = _even_out(ol, bgl, zl, hl, mod_l[2], conv_w[j], wa, wc)
            if not last:
                oc = _attention(qc, [kc], [vc])
                hc = _even_out(oc, bgc, zc, hc, mod_c[2], conv_w[j], wa, wc)
        else:
            H = ML_HEADS
            mix = mlstm_head_g.shape[1]
            dv = mix // H
            qkvg = odd_w_in.shape[2] - mix
            dk = (qkvg - 4 * H - mix) // (2 * H)
            qk = H * dk
            w = odd_w_in[j]
            win = jnp.concatenate([w[:, :2 * qk + mix], w[:, qkvg:], w[:, 2 * qk + mix:qkvg],
                                   jnp.zeros((D, LANE - 4 * H), F32)], axis=1).astype(BF16)
            gb = jnp.pad(mlstm_gate_b[j], (0, LANE - 4 * H))[None, :]
            fm = jnp.pad(jnp.tile(jnp.repeat(jnp.array([0.0, 1.0], F32), H), 2), (0, LANE - 4 * H))[None, :]
            kw = dict(qk=qk, vd=mix, q_scale=float(dk ** -0.5))
            ql, kl, vl, ogl, gtl = _odd_in(hl, g1, mod_l[0], mod_l[1], win, gb, fm, **kw)
            qc, kc, vc, ogc, gtc = _odd_in(hc, g1, mod_c[0], mod_c[1], win, gb, fm, **kw)
            gll, glc = _gate_layout(gtl, S // ML_CHUNK), _gate_layout(gtc, n_ctx // ML_CHUNK)
            hs = _mlstm(qc, kc, vc, glc, ql, kl, vl, gll, dk=dk, dv=dv)
            hg, wo = row2(mlstm_head_g[j]), odd_w_out[j].astype(BF16)
            hl = _odd_out(hs, ogl, hg, wo, hl, mod_l[2], dv=dv)
            if not last:
                raise NotImplementedError("context readout of an mLSTM layer is only needed when a layer follows it")

        nheads = peer_subkeys.shape[1]
        wqT = peer_w_q[i].T.astype(BF16)
        sub = peer_subkeys[i].reshape((2 * nheads,) + peer_subkeys.shape[3:]).astype(BF16)
        u, vT = peer_u[i].astype(BF16), peer_v[i].T.astype(BF16)
        nf = row2(norm_f_g)
        hl = _peer(hl, g2, mod_l[3], mod_l[4], mod_l[5], wqT, sub, u, vT, nf, final_norm=last)
        if not last:
            hc = _peer(hc, g2, mod_c[3], mod_c[4], mod_c[5], wqT, sub, u, vT, nf, final_norm=False)
    return hl
```
